```python
import math
import jax
import jax.numpy as jnp
from jax import lax
import numpy as np

D_MODEL = 1024
BATCH = 8
SEQ = 2048
DEPTH = 1
DEC_BATCH = 128
DEC_SEQ = 4
PAST_LEN = 8192
PAGE_SIZE = 128

A_HEADS = 8
A_KV_HEADS = 4
A_DH = D_MODEL // (2 * A_HEADS)
B_HEADS = 8
B_KV_HEADS = 4
B_DH = D_MODEL // B_HEADS
MOBA_BLOCK = 256
MOBA_TOPK = 3
MOBA_Q_CHUNK = 32
M_HEADS = 4
M_DH = D_MODEL // M_HEADS
N_MEM = 256
N_BRANCH = 3
ROPE_THETA = 500000.0
ROT_DIV = 4
ATTN_Q_BLOCK = 128
PEER_HEADS = 8
PEER_NKEYS = 128
PEER_N = PEER_NKEYS * PEER_NKEYS
PEER_DQ = 256
PEER_TOPK = 16
PEER_TOK_CHUNK = 128
EPS = 1e-6

QA_W = A_HEADS * 2 * A_DH
KA_W = A_KV_HEADS * 2 * A_DH
VA_W = A_KV_HEADS * 2 * A_DH
QB_W = B_HEADS * B_DH
KB_W = B_KV_HEADS * B_DH
VB_W = B_KV_HEADS * B_DH
QM_W = M_HEADS * M_DH
GATE_W = N_BRANCH * D_MODEL
IN_W = QA_W + KA_W + VA_W + QB_W + KB_W + VB_W + QM_W + GATE_W
IN_SPLITS = tuple(int(c) for c in np.cumsum([QA_W, KA_W, VA_W, QB_W, KB_W, VB_W, QM_W]))

kernel_name = 'hybrid_diffattn_moba_mem_peer_step'


def rms_norm(x, g):
    xf = x.astype(jnp.float32)
    y = xf * lax.rsqrt(jnp.mean(xf * xf, axis=-1, keepdims=True) + EPS)
    return (y * g.astype(jnp.float32)).astype(x.dtype)


def partial_rope(x, pos):
    d = x.shape[-1]
    r = d // ROT_DIV
    half = r // 2
    inv = ROPE_THETA ** (-jnp.arange(half, dtype=jnp.float32) * (2.0 / r))
    ang = pos.astype(jnp.float32)[:, None] * inv[None, :]
    shape = (pos.shape[0],) + (1,) * (x.ndim - 3) + (half,)
    cos = jnp.cos(ang).reshape(shape)
    sin = jnp.sin(ang).reshape(shape)
    xr = x[..., :r].astype(jnp.float32)
    x1, x2 = xr[..., :half], xr[..., half:]
    rot = jnp.concatenate([x1 * cos - x2 * sin, x2 * cos + x1 * sin], axis=-1).astype(x.dtype)
    return jnp.concatenate([rot, x[..., r:]], axis=-1)


def in_proj(x, pos, lp):
    B, T = x.shape[:2]
    h = rms_norm(x, lp['attn_norm_g'])
    qa, ka, va, qb, kb, vb, qm, gt = jnp.split(h @ lp['w_in'], IN_SPLITS, axis=-1)
    qa = partial_rope(rms_norm(qa.reshape(B, T, A_HEADS, 2, A_DH), lp['a_q_norm']), pos)
    ka = partial_rope(rms_norm(ka.reshape(B, T, A_KV_HEADS, 2, A_DH), lp['a_k_norm']), pos)
    va = va.reshape(B, T, A_KV_HEADS, 2 * A_DH)
    qb = partial_rope(rms_norm(qb.reshape(B, T, B_HEADS, B_DH), lp['b_q_norm']), pos)
    kb = partial_rope(rms_norm(kb.reshape(B, T, B_KV_HEADS, B_DH), lp['b_k_norm']), pos)
    vb = vb.reshape(B, T, B_KV_HEADS, B_DH)
    qm = rms_norm(qm.reshape(B, T, M_HEADS, M_DH), lp['m_q_norm'])
    gates = jax.nn.sigmoid(gt.astype(jnp.float32)).astype(x.dtype).reshape(B, T, N_BRANCH, D_MODEL)
    return qa, ka, va, qb, kb, vb, qm, gates


def diff_lambda(lp, lam_init):
    f = lambda a: a.astype(jnp.float32)
    return (jnp.exp(jnp.sum(f(lp['a_lq1']) * f(lp['a_lk1'])))
            - jnp.exp(jnp.sum(f(lp['a_lq2']) * f(lp['a_lk2']))) + lam_init)


def diff_attn_core(q, k, v, q_pos, k_pos, lam):
    s = jnp.einsum('btkgnd,blknd->bkgntl', q, k).astype(jnp.float32) * (A_DH ** -0.5)
    causal = k_pos[None, :] <= q_pos[:, None]
    p = jax.nn.softmax(jnp.where(causal, s, -jnp.inf), axis=-1)
    w = p[:, :, :, 0] - lam * p[:, :, :, 1]
    return jnp.einsum('bkgtl,blkv->btkgv', w.astype(v.dtype), v)


def diff_attention(q, k, v, q_pos, lam, q_block):
    B, T = q.shape[:2]
    G = A_HEADS // A_KV_HEADS
    n = T // q_block
    qg = q.reshape(B, n, q_block, A_KV_HEADS, G, 2, A_DH).swapaxes(0, 1)
    k_pos = jnp.arange(k.shape[1], dtype=jnp.int32)
    o = lax.map(lambda a: diff_attn_core(a[0], k, v, a[1], k_pos, lam), (qg, q_pos.reshape(n, q_block)))
    return o.swapaxes(0, 1).reshape(B, T, A_HEADS, 2 * A_DH)


def moba_blocks(k, v):
    B, L, Hk, dh = k.shape
    nb = -(-L // MOBA_BLOCK)
    pad = ((0, 0), (0, nb * MOBA_BLOCK - L), (0, 0), (0, 0))
    kr = jnp.pad(k, pad).reshape(B, nb, MOBA_BLOCK, Hk, dh)
    vr = jnp.pad(v, pad).reshape(B, nb, MOBA_BLOCK, Hk, dh)
    kmean = jnp.mean(kr.astype(jnp.float32), axis=2)
    return kr, vr, kmean


def moba_core(q, q_pos, kr, vr, kmean):
    B, Tq, H, dh = q.shape
    nb = kr.shape[1]
    kvh = jnp.arange(H) // (H // B_KV_HEADS)
    gate = jnp.einsum('bthd,bnhd->bthn', q.astype(jnp.float32), kmean[:, :, kvh, :])
    own = q_pos // MOBA_BLOCK
    fully_past = jnp.arange(nb)[None, :] < own[:, None]
    gate = jnp.where(fully_past[None, :, None, :], gate, -jnp.inf)
    top_s, top_i = lax.top_k(gate, min(MOBA_TOPK, nb))
    blk = jnp.concatenate([top_i, jnp.broadcast_to(own[None, :, None, None], (B, Tq, H, 1)).astype(top_i.dtype)], axis=-1)
    ok = jnp.concatenate([jnp.isfinite(top_s), jnp.ones((B, Tq, H, 1), dtype=bool)], axis=-1)
    kvh_idx = kvh[None, :, None]
    kg = jax.vmap(lambda a, i: a[i, :, kvh_idx, :])(kr, blk)
    vg = jax.vmap(lambda a, i: a[i, :, kvh_idx, :])(vr, blk)
    s = jnp.einsum('bthd,bthjsd->bthjs', q, kg).astype(jnp.float32) * (dh ** -0.5)
    kpos = blk[..., None] * MOBA_BLOCK + jnp.arange(MOBA_BLOCK)
    mask = ok[..., None] & (kpos <= q_pos[None, :, None, None, None])
    s = jnp.where(mask, s, -jnp.inf)
    p = jax.nn.softmax(s.reshape(B, Tq, H, -1), axis=-1).reshape(s.shape)
    return jnp.einsum('bthjs,bthjsd->bthd', p.astype(vg.dtype), vg)


def moba_attention(q, q_pos, kr, vr, kmean, chunk):
    B, T, H, dh = q.shape
    n = T // chunk
    qc = q.reshape(B, n, chunk, H, dh).swapaxes(0, 1)
    o = lax.map(lambda a: moba_core(a[0], a[1], kr, vr, kmean), (qc, q_pos.reshape(n, chunk)))
    return o.swapaxes(0, 1).reshape(B, T, H * dh)


def mem_kv(mem, lp):
    B = mem.shape[0]
    kv = rms_norm(mem, lp['mem_norm_g']) @ lp['w_mem_kv']
    k, v = jnp.split(kv, 2, axis=-1)
    k = rms_norm(k.reshape(B, N_MEM, M_HEADS, M_DH), lp['m_k_norm'])
    return k, v.reshape(B, N_MEM, M_HEADS, M_DH)


def mem_attend(q, k, v):
    B, T = q.shape[:2]
    s = jnp.einsum('bthd,bnhd->bhtn', q, k).astype(jnp.float32) * (M_DH ** -0.5)
    p = jax.nn.softmax(s, axis=-1)
    return jnp.einsum('bhtn,bnhd->bthd', p.astype(v.dtype), v).reshape(B, T, QM_W)


def peer_ffn(h, w_q, keys, u_tab, v_tab):
    B, T, D = h.shape
    n = B * T
    nc = -(-n // PEER_TOK_CHUNK)
    xt = jnp.pad(h.reshape(n, D), ((0, nc * PEER_TOK_CHUNK - n), (0, 0))).reshape(nc, PEER_TOK_CHUNK, D)
    C = PEER_TOK_CHUNK

    def one_chunk(xc):
        q = (xc @ w_q).reshape(C, PEER_HEADS, 2, PEER_DQ // 2)
        s = jnp.einsum('chpd,hpkd->chpk', q, keys).astype(jnp.float32)
        sub_s, sub_i = lax.top_k(s, PEER_TOPK)
        cand_s = (sub_s[:, :, 0, :, None] + sub_s[:, :, 1, None, :]).reshape(C, PEER_HEADS, PEER_TOPK * PEER_TOPK)
        cand_i = (sub_i[:, :, 0, :, None] * PEER_NKEYS + sub_i[:, :, 1, None, :]).reshape(C, PEER_HEADS, PEER_TOPK * PEER_TOPK)
        top_s, top_j = lax.top_k(cand_s, PEER_TOPK)
        expert = jnp.take_along_axis(cand_i, top_j, axis=-1)
        g = jax.nn.softmax(top_s, axis=-1)
        act = jax.nn.gelu(jnp.einsum('chkd,cd->chk', u_tab[expert], xc).astype(jnp.float32), approximate=False)
        return jnp.einsum('chk,chkd->cd', (g * act).astype(xc.dtype), v_tab[expert])

    return lax.map(one_chunk, xt).reshape(-1, D)[:n].reshape(B, T, D)


def merge_ffn(x, oa, ob, om, gates, lp):
    B, T = x.shape[:2]
    ya = oa.reshape(B, T, QA_W) @ lp['w_br_a']
    yb = ob @ lp['w_br_b']
    ym = om @ lp['w_br_m']
    merged = gates[:, :, 0] * ya + gates[:, :, 1] * yb + gates[:, :, 2] * ym
    x = x + merged @ lp['w_out']
    return x + peer_ffn(rms_norm(x, lp['ffn_norm_g']), lp['w_peer_q'], lp['peer_keys'], lp['peer_u'], lp['peer_v'])


def prompt_layer(x, mem, lp, lam_init):
    S = x.shape[1]
    pos = jnp.arange(S, dtype=jnp.int32)
    qa, ka, va, qb, kb, vb, qm, gates = in_proj(x, pos, lp)
    lam = diff_lambda(lp, lam_init)
    oa = rms_norm(diff_attention(qa, ka, va, pos, lam, ATTN_Q_BLOCK), lp['a_subln']) * (1.0 - lam_init)
    kr, vr, kmean = moba_blocks(kb, vb)
    ob = moba_attention(qb, pos, kr, vr, kmean, MOBA_Q_CHUNK)
    mk, mv = mem_kv(mem, lp)
    om = mem_attend(qm, mk, mv)
    y = merge_ffn(x, oa, ob, om, gates, lp)
    return y, ka, va, kb, vb, mk, mv


def gather_pages(pool, layer, page_table):
    g = pool[layer, page_table]
    return g.reshape((page_table.shape[0], page_table.shape[1] * pool.shape[2]) + pool.shape[3:])


def sample_layer(x, cache_a_k, cache_a_v, cache_b_k, cache_b_v, mem_k, mem_v, page_table, layer, lp, lam_init):
    T = x.shape[1]
    past = page_table.shape[1] * PAGE_SIZE
    pos = past + jnp.arange(T, dtype=jnp.int32)
    qa, ka, va, qb, kb, vb, qm, gates = in_proj(x, pos, lp)
    lam = diff_lambda(lp, lam_init)
    ka_all = jnp.concatenate([gather_pages(cache_a_k, layer, page_table), ka], axis=1)
    va_all = jnp.concatenate([gather_pages(cache_a_v, layer, page_table), va], axis=1)
    oa = rms_norm(diff_attention(qa, ka_all, va_all, pos, lam, T), lp['a_subln']) * (1.0 - lam_init)
    kr, vr, kmean = moba_blocks(jnp.concatenate([gather_pages(cache_b_k, layer, page_table), kb], axis=1),
                                jnp.concatenate([gather_pages(cache_b_v, layer, page_table), vb], axis=1))
    ob = moba_attention(qb, pos, kr, vr, kmean, 1)
    om = mem_attend(qm, mem_k, mem_v)
    y = merge_ffn(x, oa, ob, om, gates, lp)
    return y, ka, va, kb, vb


def setup_inputs(seed: int = 0) -> dict:
    key = jax.random.key(seed)
    k = jax.random.split(key, 40)
    f32 = jnp.float32
    nrm = lambda kk, shape, scale: jax.random.normal(kk, shape, f32) * scale
    gain = lambda kk, shape: 1.0 + 0.01 * jax.random.normal(kk, shape, f32)
    n_pages = PAST_LEN // PAGE_SIZE
    used = DEC_BATCH * n_pages
    n_pool = used + max(1, used // 4)
    page_table = jax.random.permutation(k[0], n_pool)[:used].reshape(DEC_BATCH, n_pages).astype(jnp.int32)
    L = DEPTH
    return {
        'x_prompt': nrm(k[1], (BATCH, SEQ, D_MODEL), 1.0),
        'x_sample': nrm(k[2], (DEC_BATCH, DEC_SEQ, D_MODEL), 1.0),
        'mem_prompt': nrm(k[3], (BATCH, N_MEM, D_MODEL), 1.0),
        'cache_a_k': nrm(k[4], (L, n_pool, PAGE_SIZE, A_KV_HEADS, 2, A_DH), 1.0),
        'cache_a_v': nrm(k[5], (L, n_pool, PAGE_SIZE, A_KV_HEADS, 2 * A_DH), 1.0),
        'cache_b_k': nrm(k[6], (L, n_pool, PAGE_SIZE, B_KV_HEADS, B_DH), 1.0),
        'cache_b_v': nrm(k[7], (L, n_pool, PAGE_SIZE, B_KV_HEADS, B_DH), 1.0),
        'cache_mem_k': nrm(k[8], (L, DEC_BATCH, N_MEM, M_HEADS, M_DH), 1.0),
        'cache_mem_v': nrm(k[9], (L, DEC_BATCH, N_MEM, M_HEADS, M_DH), 1.0),
        'page_table': page_table,
        'attn_norm_g': gain(k[10], (L, D_MODEL)),
        'w_in': nrm(k[11], (L, D_MODEL, IN_W), D_MODEL ** -0.5),
        'a_q_norm': gain(k[12], (L, A_DH)),
        'a_k_norm': gain(k[13], (L, A_DH)),
        'a_lq1': nrm(k[14], (L, A_DH), 0.1),
        'a_lk1': nrm(k[15], (L, A_DH), 0.1),
        'a_lq2': nrm(k[16], (L, A_DH), 0.1),
        'a_lk2': nrm(k[17], (L, A_DH), 0.1),
        'a_subln': gain(k[18], (L, 2 * A_DH)),
        'b_q_norm': gain(k[19], (L, B_DH)),
        'b_k_norm': gain(k[20], (L, B_DH)),
        'mem_norm_g': gain(k[21], (L, D_MODEL)),
        'w_mem_kv': nrm(k[22], (L, D_MODEL, 2 * QM_W), D_MODEL ** -0.5),
        'm_q_norm': gain(k[23], (L, M_DH)),
        'm_k_norm': gain(k[24], (L, M_DH)),
        'w_br_a': nrm(k[25], (L, QA_W, D_MODEL), QA_W ** -0.5),
        'w_br_b': nrm(k[26], (L, QB_W, D_MODEL), QB_W ** -0.5),
        'w_br_m': nrm(k[27], (L, QM_W, D_MODEL), QM_W ** -0.5),
        'w_out': nrm(k[28], (L, D_MODEL, D_MODEL), D_MODEL ** -0.5),
        'ffn_norm_g': gain(k[29], (L, D_MODEL)),
        'w_peer_q': nrm(k[30], (L, D_MODEL, PEER_HEADS * PEER_DQ), D_MODEL ** -0.5),
        'peer_keys': nrm(k[31], (L, PEER_HEADS, 2, PEER_NKEYS, PEER_DQ // 2), (PEER_DQ // 2) ** -0.5),
        'peer_u': nrm(k[32], (L, PEER_N, D_MODEL), D_MODEL ** -0.5),
        'peer_v': nrm(k[33], (L, PEER_N, D_MODEL), PEER_HEADS ** -0.5),
    }


def reference(x_prompt, x_sample, mem_prompt, cache_a_k, cache_a_v, cache_b_k, cache_b_v,
              cache_mem_k, cache_mem_v, page_table, attn_norm_g, w_in, a_q_norm, a_k_norm,
              a_lq1, a_lk1, a_lq2, a_lk2, a_subln, b_q_norm, b_k_norm, mem_norm_g, w_mem_kv,
              m_q_norm, m_k_norm, w_br_a, w_br_b, w_br_m, w_out, ffn_norm_g, w_peer_q,
              peer_keys, peer_u, peer_v):
    xp, xs = x_prompt, x_sample
    pa_k, pa_v, pb_k, pb_v, pm_k, pm_v = [], [], [], [], [], []
    sa_k, sa_v, sb_k, sb_v = [], [], [], []
    for i in range(DEPTH):
        lp = dict(attn_norm_g=attn_norm_g[i], w_in=w_in[i], a_q_norm=a_q_norm[i], a_k_norm=a_k_norm[i],
                  a_lq1=a_lq1[i], a_lk1=a_lk1[i], a_lq2=a_lq2[i], a_lk2=a_lk2[i], a_subln=a_subln[i],
                  b_q_norm=b_q_norm[i], b_k_norm=b_k_norm[i], mem_norm_g=mem_norm_g[i],
                  w_mem_kv=w_mem_kv[i], m_q_norm=m_q_norm[i], m_k_norm=m_k_norm[i],
                  w_br_a=w_br_a[i], w_br_b=w_br_b[i], w_br_m=w_br_m[i], w_out=w_out[i],
                  ffn_norm_g=ffn_norm_g[i], w_peer_q=w_peer_q[i], peer_keys=peer_keys[i],
                  peer_u=peer_u[i], peer_v=peer_v[i])
        lam_init = 0.8 - 0.6 * math.exp(-0.3 * i)
        xp, ka, va, kb, vb, mk, mv = prompt_layer(xp, mem_prompt, lp, lam_init)
        pa_k.append(ka); pa_v.append(va); pb_k.append(kb); pb_v.append(vb); pm_k.append(mk); pm_v.append(mv)
        xs, ka, va, kb, vb = sample_layer(xs, cache_a_k, cache_a_v, cache_b_k, cache_b_v,
                                          cache_mem_k[i], cache_mem_v[i], page_table, i, lp, lam_init)
        sa_k.append(ka); sa_v.append(va); sb_k.append(kb); sb_v.append(vb)
    return (xp, xs, jnp.stack(pa_k), jnp.stack(pa_v), jnp.stack(pb_k), jnp.stack(pb_v),
            jnp.stack(pm_k), jnp.stack(pm_v), jnp.stack(sa_k), jnp.stack(sa_v),
            jnp.stack(sb_k), jnp.stack(sb_v))
```

```python
import functools
import math

import jax
import jax.numpy as jnp
from jax import lax
from jax.experimental import pallas as pl
from jax.experimental.pallas import tpu as pltpu

F32 = jnp.float32
BF16 = jnp.bfloat16
I32 = jnp.int32
NEG_INF = float("-inf")

D_MODEL = 1024
PAGE_SIZE = 128
A_HEADS, A_KV_HEADS, A_DH = 8, 4, 64
B_HEADS, B_KV_HEADS, B_DH = 8, 4, 128
MOBA_BLOCK, MOBA_TOPK = 256, 3
M_HEADS, M_DH = 4, 256
ROPE_THETA, ROT_DIV = 500000.0, 4
PEER_HEADS, PEER_NKEYS, PEER_DQ, PEER_TOPK = 8, 128, 256, 16
PEER_N = PEER_NKEYS * PEER_NKEYS
EPS = 1e-6
LAM_INIT = 0.8 - 0.6 * math.exp(-0.3 * 0)

LANES = 128
KV_W = 512
Q_W = 1024
GATE_W = 3 * D_MODEL
IN_W = 3 * Q_W + 4 * KV_W + GATE_W
PEER_SLAB = 4
VMEM_LIMIT = 56 * 1024 * 1024

_NT = (((1,), (1,)), ((), ()))


def _const_spec(shape, single_buffer=False):
    index_map = lambda *_: (0,) * len(shape)
    if single_buffer:
        return pl.BlockSpec(shape, index_map, pipeline_mode=pl.Buffered(1))
    return pl.BlockSpec(shape, index_map)


def _params(sem):
    return pltpu.CompilerParams(dimension_semantics=sem, vmem_limit_bytes=VMEM_LIMIT)


def _rms(x, g, width):
    return x * lax.rsqrt(jnp.sum(x * x, axis=-1, keepdims=True) * (1.0 / width) + EPS) * g


def _rope_table(pos, seg):
    r = seg // ROT_DIV
    half = r // 2
    inv = ROPE_THETA ** (-jnp.arange(half, dtype=F32) * (2.0 / r))
    ang = pos.astype(F32)[:, None] * inv[None, :]
    cos, sin = jnp.cos(ang), jnp.sin(ang)
    n = pos.shape[0]
    z_half = jnp.zeros((n, half), F32)
    z_rest = jnp.zeros((n, seg - r), F32)
    c = jnp.concatenate([cos, cos, jnp.ones((n, seg - r), F32)], axis=1)
    sm = jnp.concatenate([-sin, z_half, z_rest], axis=1)
    sp = jnp.concatenate([z_half, sin, z_rest], axis=1)
    reps = LANES // seg
    return jnp.concatenate([jnp.tile(c, (1, reps)), jnp.tile(sm, (1, reps)), jnp.tile(sp, (1, reps))], axis=1)


def _rope(y, tab_ref, half):
    c = tab_ref[:, 0:LANES]
    sm = tab_ref[:, LANES:2 * LANES]
    sp = tab_ref[:, 2 * LANES:3 * LANES]
    return y * c + pltpu.roll(y, LANES - half, 1) * sm + pltpu.roll(y, half, 1) * sp


def _in_proj_kernel(x_ref, g_ref, w_ref, ropea_ref, ropeb_ref, aqn_ref, akn_ref, bqn_ref, bkn_ref, mqn_ref,
                    qa_ref, ka_ref, va_ref, qb_ref, kb_ref, vb_ref, qm_ref, gt_ref):
    x = x_ref[...]
    hb = _rms(x, g_ref[...], D_MODEL).astype(BF16)
    tm = x.shape[0]
    lane = lax.broadcasted_iota(I32, (tm, LANES), 1)
    lo = lane < A_DH

    def mm(c0, width):
        return jnp.dot(hb, w_ref[:, c0:c0 + width], preferred_element_type=F32)

    def norm64(y, gn):
        ss = y * y
        s_lo = jnp.sum(jnp.where(lo, ss, 0.0), axis=-1, keepdims=True)
        s_hi = jnp.sum(jnp.where(lo, 0.0, ss), axis=-1, keepdims=True)
        inv = jnp.where(lo, lax.rsqrt(s_lo * (1.0 / A_DH) + EPS), lax.rsqrt(s_hi * (1.0 / A_DH) + EPS))
        return y * inv * gn

    c0 = 0
    y = mm(c0, Q_W)
    for b in range(Q_W // LANES):
        blk = slice(b * LANES, (b + 1) * LANES)
        qa_ref[:, blk] = (_rope(norm64(y[:, blk], aqn_ref[...]), ropea_ref, A_DH // ROT_DIV // 2)
                          * (A_DH ** -0.5)).astype(qa_ref.dtype)
    c0 += Q_W
    y = mm(c0, KV_W)
    for b in range(KV_W // LANES):
        blk = slice(b * LANES, (b + 1) * LANES)
        ka_ref[:, blk] = _rope(norm64(y[:, blk], akn_ref[...]), ropea_ref, A_DH // ROT_DIV // 2)
    c0 += KV_W
    va_ref[...] = mm(c0, KV_W)
    c0 += KV_W
    y = mm(c0, Q_W)
    for b in range(Q_W // LANES):
        blk = slice(b * LANES, (b + 1) * LANES)
        qb_ref[:, blk] = _rope(_rms(y[:, blk], bqn_ref[...], B_DH), ropeb_ref, B_DH // ROT_DIV // 2)
    c0 += Q_W
    y = mm(c0, KV_W)
    for b in range(KV_W // LANES):
        blk = slice(b * LANES, (b + 1) * LANES)
        kb_ref[:, blk] = _rope(_rms(y[:, blk], bkn_ref[...], B_DH), ropeb_ref, B_DH // ROT_DIV // 2)
    c0 += KV_W
    vb_ref[...] = mm(c0, KV_W)
    c0 += KV_W
    y = mm(c0, Q_W)
    for h in range(M_HEADS):
        blk = slice(h * M_DH, (h + 1) * M_DH)
        qm_ref[:, blk] = (_rms(y[:, blk], mqn_ref[...], M_DH) * (M_DH ** -0.5)).astype(qm_ref.dtype)
    c0 += Q_W
    for b in range(3):
        gt_ref[:, b * D_MODEL:(b + 1) * D_MODEL] = jax.nn.sigmoid(mm(c0 + b * D_MODEL, D_MODEL))


def _in_proj(x2d, rope_a, rope_b, prm, *, tm, q_dtype):
    n = x2d.shape[0]
    n_rope = rope_a.shape[0] // tm
    row = lambda w: pl.BlockSpec((tm, w), lambda i: (i, 0))
    rope_spec = pl.BlockSpec((tm, 3 * LANES), lambda i: (i % n_rope, 0))
    out_shape = [jax.ShapeDtypeStruct((n, Q_W), q_dtype), jax.ShapeDtypeStruct((n, KV_W), F32),
                 jax.ShapeDtypeStruct((n, KV_W), F32), jax.ShapeDtypeStruct((n, Q_W), F32),
                 jax.ShapeDtypeStruct((n, KV_W), F32), jax.ShapeDtypeStruct((n, KV_W), F32),
                 jax.ShapeDtypeStruct((n, Q_W), q_dtype), jax.ShapeDtypeStruct((n, GATE_W), F32)]
    return pl.pallas_call(
        _in_proj_kernel,
        grid=(n // tm,),
        in_specs=[row(D_MODEL), _const_spec((1, D_MODEL)), _const_spec((D_MODEL, IN_W), True), rope_spec, rope_spec,
                  _const_spec((1, LANES)), _const_spec((1, LANES)), _const_spec((1, LANES)), _const_spec((1, LANES)),
                  _const_spec((1, M_DH))],
        out_specs=[row(Q_W), row(KV_W), row(KV_W), row(Q_W), row(KV_W), row(KV_W), row(Q_W), row(GATE_W)],
        out_shape=out_shape,
        compiler_params=_params(("arbitrary",)),
        name="in_proj",
    )(x2d, prm["attn_norm_g"], prm["w_in"], rope_a, rope_b, prm["a_q_norm"], prm["a_k_norm"],
      prm["b_q_norm"], prm["b_k_norm"], prm["m_q_norm"])


def _diff_lambda(lq1_ref, lk1_ref, lq2_ref, lk2_ref):
    return (jnp.exp(jnp.sum(lq1_ref[...] * lk1_ref[...], keepdims=True))
            - jnp.exp(jnp.sum(lq2_ref[...] * lk2_ref[...], keepdims=True)) + LAM_INIT)


def _softmax_step(carry, s, v):
    m, l, acc = carry
    m_new = jnp.maximum(m, jnp.max(s, axis=-1, keepdims=True))
    alpha = jnp.exp(m - m_new)
    p = jnp.exp(s - m_new)
    l = alpha * l + jnp.sum(p, axis=-1, keepdims=True)
    acc = alpha * acc + jnp.dot(p.astype(BF16), v, preferred_element_type=F32)
    return m_new, l, acc


def _softmax_init(rows, width):
    return (jnp.full((rows, 1), NEG_INF, F32), jnp.zeros((rows, 1), F32), jnp.zeros((rows, width), F32))


def _diff_attn_kernel(lq1_ref, lk1_ref, lq2_ref, lk2_ref, q_ref, k_ref, v_ref, subln_ref, o_ref, kb_sc, vb_sc, *, tq):
    qi = pl.program_id(2)

    @pl.when(qi == 0)
    def _cast_kv():
        kb_sc[...] = k_ref[...].astype(BF16)
        vb_sc[...] = v_ref[...].astype(BF16)

    lam = _diff_lambda(lq1_ref, lk1_ref, lq2_ref, lk2_ref)
    lane = lax.broadcasted_iota(I32, (tq, LANES), 1)
    r = lax.broadcasted_iota(I32, (2 * tq, tq), 0)
    r = jnp.where(r >= tq, r - tq, r)
    causal = r >= lax.broadcasted_iota(I32, (2 * tq, tq), 1)

    for g in range(A_HEADS // A_KV_HEADS):
        qg = q_ref[:, g * LANES:(g + 1) * LANES]
        zero = jnp.zeros_like(qg)
        qq = jnp.concatenate([jnp.where(lane < A_DH, qg, zero), jnp.where(lane >= A_DH, qg, zero)], axis=0)

        def step(j, carry, masked):
            kj = kb_sc[pl.ds(pl.multiple_of(j * tq, tq), tq), :]
            vj = vb_sc[pl.ds(pl.multiple_of(j * tq, tq), tq), :]
            s = lax.dot_general(qq, kj, _NT, preferred_element_type=F32)
            if masked:
                s = jnp.where(causal, s, NEG_INF)
            return _softmax_step(carry, s, vj)

        carry = step(qi, _softmax_init(2 * tq, LANES), True)
        m, l, acc = lax.fori_loop(0, qi, lambda j, c: step(j, c, False), carry)
        o = acc / l
        o = o[:tq] - lam * o[tq:]
        o = _rms(o, subln_ref[...], 2 * A_DH) * (1.0 - LAM_INIT)
        o_ref[:, g * LANES:(g + 1) * LANES] = o.astype(o_ref.dtype)


def _diff_attn_prompt(qa, ka, va, prm, *, tq):
    bsz, seq = qa.shape[:2]
    grp = (A_HEADS // A_KV_HEADS) * LANES
    lam_spec = _const_spec((1, A_DH))
    return pl.pallas_call(
        functools.partial(_diff_attn_kernel, tq=tq),
        grid=(bsz, A_KV_HEADS, seq // tq),
        in_specs=[lam_spec, lam_spec, lam_spec, lam_spec,
                  pl.BlockSpec((None, tq, grp), lambda b, h, i: (b, i, h)),
                  pl.BlockSpec((None, seq, LANES), lambda b, h, i: (b, 0, h)),
                  pl.BlockSpec((None, seq, LANES), lambda b, h, i: (b, 0, h)),
                  _const_spec((1, LANES))],
        out_specs=pl.BlockSpec((None, tq, grp), lambda b, h, i: (b, i, h)),
        out_shape=jax.ShapeDtypeStruct((bsz, seq, Q_W), BF16),
        scratch_shapes=[pltpu.VMEM((seq, LANES), BF16), pltpu.VMEM((seq, LANES), BF16)],
        compiler_params=_params(("arbitrary", "arbitrary", "arbitrary")),
        name="diff_attn_prompt",
    )(prm["a_lq1"], prm["a_lk1"], prm["a_lq2"], prm["a_lk2"], qa, ka, va, prm["a_subln"])


def _select_topk_lanes(gate, k):
    col = lax.broadcasted_iota(I32, gate.shape, 1).astype(F32)
    sel = jnp.zeros(gate.shape, jnp.bool_)
    for _ in range(k):
        mx = jnp.max(gate, axis=-1, keepdims=True)
        ix = jnp.min(jnp.where(gate == mx, col, float(LANES)), axis=-1, keepdims=True)
        hit = col == ix
        sel = sel | (hit & (mx > NEG_INF))
        gate = jnp.where(hit, NEG_INF, gate)
    return sel


def _moba_kernel(q_ref, k_ref, v_ref, o_ref, kb_sc, vb_sc, km_sc, *, tq, n_blocks):
    qi = pl.program_id(2)

    @pl.when(qi == 0)
    def _prep_kv():
        kb_sc[...] = k_ref[...].astype(BF16)
        vb_sc[...] = v_ref[...].astype(BF16)
        km_sc[...] = jnp.zeros_like(km_sc)
        for j in range(n_blocks):
            km_sc[j:j + 1, :] = jnp.mean(k_ref[j * MOBA_BLOCK:(j + 1) * MOBA_BLOCK, :], axis=0, keepdims=True)

    n_grp = B_HEADS // B_KV_HEADS
    qf = jnp.concatenate([q_ref[:, g * LANES:(g + 1) * LANES] for g in range(n_grp)], axis=0)
    rows = n_grp * tq
    gate = lax.dot_general(qf, km_sc[...], _NT, precision=lax.Precision.HIGHEST, preferred_element_type=F32)
    col = lax.broadcasted_iota(I32, (rows, LANES), 1)
    gate = jnp.where(col < qi, gate, NEG_INF)
    sel = _select_topk_lanes(gate, MOBA_TOPK).astype(F32)

    qb = qf.astype(BF16)
    r = lax.broadcasted_iota(I32, (rows, tq), 0)
    for g in range(1, n_grp):
        r = jnp.where(r >= g * tq, r - tq, r)
    causal = r >= lax.broadcasted_iota(I32, (rows, tq), 1)
    scale = B_DH ** -0.5

    def step(j, carry, own):
        kj = kb_sc[pl.ds(pl.multiple_of(j * tq, tq), tq), :]
        vj = vb_sc[pl.ds(pl.multiple_of(j * tq, tq), tq), :]
        s = lax.dot_general(qb, kj, _NT, preferred_element_type=F32) * scale
        if own:
            s = jnp.where(causal, s, NEG_INF)
        else:
            sel_j = jnp.sum(jnp.where(col == j, sel, 0.0), axis=-1, keepdims=True) > 0.0
            s = jnp.where(sel_j, s, NEG_INF)
        return _softmax_step(carry, s, vj)

    carry = step(qi, _softmax_init(rows, LANES), True)
    m, l, acc = lax.fori_loop(0, qi, lambda j, c: step(j, c, False), carry)
    o = acc / l
    for g in range(n_grp):
        o_ref[:, g * LANES:(g + 1) * LANES] = o[g * tq:(g + 1) * tq].astype(o_ref.dtype)


def _moba_prompt(qb, kb, vb):
    bsz, seq = qb.shape[:2]
    tq = MOBA_BLOCK
    n_blocks = seq // MOBA_BLOCK
    assert seq % MOBA_BLOCK == 0 and n_blocks <= LANES
    grp = (B_HEADS // B_KV_HEADS) * LANES
    return pl.pallas_call(
        functools.partial(_moba_kernel, tq=tq, n_blocks=n_blocks),
        grid=(bsz, B_KV_HEADS, seq // tq),
        in_specs=[pl.BlockSpec((None, tq, grp), lambda b, h, i: (b, i, h)),
                  pl.BlockSpec((None, seq, LANES), lambda b, h, i: (b, 0, h)),
                  pl.BlockSpec((None, seq, LANES), lambda b, h, i: (b, 0, h))],
        out_specs=pl.BlockSpec((None, tq, grp), lambda b, h, i: (b, i, h)),
        out_shape=jax.ShapeDtypeStruct((bsz, seq, Q_W), BF16),
        scratch_shapes=[pltpu.VMEM((seq, LANES), BF16), pltpu.VMEM((seq, LANES), BF16), pltpu.VMEM((LANES, LANES), F32)],
        compiler_params=_params(("arbitrary", "arbitrary", "arbitrary")),
        name="moba_prompt",
    )(qb, kb, vb)


def _mem_kv_kernel(x_ref, g_ref, w_ref, kn_ref, k_ref, v_ref):
    hb = _rms(x_ref[...], g_ref[...], D_MODEL).astype(BF16)
    y = jnp.dot(hb, w_ref[:, 0:Q_W], preferred_element_type=F32)
    for h in range(M_HEADS):
        blk = slice(h * M_DH, (h + 1) * M_DH)
        k_ref[:, blk] = _rms(y[:, blk], kn_ref[...], M_DH)
    v_ref[...] = jnp.dot(hb, w_ref[:, Q_W:2 * Q_W], preferred_element_type=F32)


def _mem_kv(mem2d, prm, *, tm):
    n = mem2d.shape[0]
    row = pl.BlockSpec((tm, D_MODEL), lambda i: (i, 0))
    return pl.pallas_call(
        _mem_kv_kernel,
        grid=(n // tm,),
        in_specs=[row, _const_spec((1, D_MODEL)), _const_spec((D_MODEL, 2 * Q_W)), _const_spec((1, M_DH))],
        out_specs=[row, row],
        out_shape=[jax.ShapeDtypeStruct((n, Q_W), F32), jax.ShapeDtypeStruct((n, Q_W), F32)],
        compiler_params=_params(("arbitrary",)),
        name="mem_kv",
    )(mem2d, prm["mem_norm_g"], prm["w_mem_kv"], prm["m_k_norm"])


def _mem_attn_kernel(q_ref, k_ref, v_ref, o_ref):
    for h in range(M_HEADS):
        blk = slice(h * M_DH, (h + 1) * M_DH)
        q = q_ref[:, blk].astype(BF16)
        s = lax.dot_general(q, k_ref[:, blk].astype(BF16), _NT, preferred_element_type=F32)
        p = jnp.exp(s - jnp.max(s, axis=-1, keepdims=True))
        p = p / jnp.sum(p, axis=-1, keepdims=True)
        o_ref[:, blk] = jnp.dot(p.astype(BF16), v_ref[:, blk].astype(BF16), preferred_element_type=F32).astype(o_ref.dtype)


def _mem_attn(qm, mk, mv, *, tq, out_dtype):
    bsz, seq = qm.shape[:2]
    n_mem = mk.shape[1]
    return pl.pallas_call(
        _mem_attn_kernel,
        grid=(bsz, seq // tq),
        in_specs=[pl.BlockSpec((None, tq, Q_W), lambda b, i: (b, i, 0)),
                  pl.BlockSpec((None, n_mem, Q_W), lambda b, i: (b, 0, 0)),
                  pl.BlockSpec((None, n_mem, Q_W), lambda b, i: (b, 0, 0))],
        out_specs=pl.BlockSpec((None, tq, Q_W), lambda b, i: (b, i, 0)),
        out_shape=jax.ShapeDtypeStruct((bsz, seq, Q_W), out_dtype),
        compiler_params=_params(("arbitrary", "arbitrary")),
        name="mem_attn",
    )(qm, mk, mv)


def _paged_specs(n_page_ops, n_steps):
    def spec(p):
        return pl.BlockSpec((None, PAGE_SIZE, KV_W), lambda b, s, pt: (pt[b * (n_steps * n_page_ops) + s * n_page_ops + p], 0, 0))
    return [spec(p) for p in range(n_page_ops)]


def _new_key_mask(rows, n_new, t_of_row):
    col = lax.broadcasted_iota(I32, (rows, 8), 1)
    return (col <= t_of_row) & (col < n_new)


def _diff_decode_kernel(pt_ref, lq1_ref, lk1_ref, lq2_ref, lk2_ref, q_ref, kn_ref, vn_ref, subln_ref, *rest, n_pages, n_new):
    k_refs, v_refs = rest[:n_pages], rest[n_pages:2 * n_pages]
    o_ref, m_sc, l_sc, acc_sc = rest[2 * n_pages:]
    s_id = pl.program_id(1)
    q = q_ref[...]
    rows = q.shape[0]

    @pl.when(s_id == 0)
    def _init():
        m_sc[...] = jnp.full(m_sc.shape, NEG_INF, F32)
        l_sc[...] = jnp.zeros(l_sc.shape, F32)
        acc_sc[...] = jnp.zeros(acc_sc.shape, F32)

    kk = jnp.concatenate([r[...] for r in k_refs], axis=0).astype(BF16)
    vv = jnp.concatenate([r[...] for r in v_refs], axis=0).astype(BF16)
    s = lax.dot_general(q, kk, _NT, preferred_element_type=F32)
    m, l, acc = _softmax_step((m_sc[...], l_sc[...], acc_sc[...]), s, vv)
    m_sc[...] = m
    l_sc[...] = l
    acc_sc[...] = acc

    @pl.when(s_id == pl.num_programs(1) - 1)
    def _finish():
        lam = _diff_lambda(lq1_ref, lk1_ref, lq2_ref, lk2_ref)
        t_of_row = lax.broadcasted_iota(I32, (rows, 8), 0) % n_new
        s_new = lax.dot_general(q, kn_ref[...].astype(BF16), _NT, preferred_element_type=F32)
        s_new = jnp.where(_new_key_mask(rows, n_new, t_of_row), s_new, NEG_INF)
        m2, l2, acc2 = _softmax_step((m_sc[...], l_sc[...], acc_sc[...]), s_new, vn_ref[...].astype(BF16))
        o = acc2 / l2
        n_grp = A_HEADS // A_KV_HEADS
        per_map = n_grp * n_new
        for kvh in range(A_KV_HEADS):
            base = kvh * 2 * per_map
            cols = slice(kvh * LANES, (kvh + 1) * LANES)
            od = o[base:base + per_map, cols] - lam * o[base + per_map:base + 2 * per_map, cols]
            od = _rms(od, subln_ref[...], 2 * A_DH) * (1.0 - LAM_INIT)
            for g in range(n_grp):
                h = kvh * n_grp + g
                o_ref[:, h * LANES:(h + 1) * LANES] = od[g * n_new:(g + 1) * n_new]


def _moba_decode_kernel(pt_ref, q_ref, kn_ref, vn_ref, *rest, n_pages, n_new, n_past_blocks):
    k_refs, v_refs = rest[:n_pages], rest[n_pages:2 * n_pages]
    o_ref, m_sc, l_sc, o_sc, km_sc = rest[2 * n_pages:]
    s_id = pl.program_id(1)
    qf = q_ref[...]
    qb = qf.astype(BF16)
    rows = qf.shape[0]
    scale = B_DH ** -0.5
    pages_per_block = MOBA_BLOCK // PAGE_SIZE
    blocks_per_step = n_pages // pages_per_block

    @pl.when(s_id == 0)
    def _init():
        km_sc[...] = jnp.zeros_like(km_sc)

    for jj in range(blocks_per_step):
        j = s_id * blocks_per_step + jj
        kf = jnp.concatenate([k_refs[jj * pages_per_block + p][...] for p in range(pages_per_block)], axis=0)
        vb = jnp.concatenate([v_refs[jj * pages_per_block + p][...] for p in range(pages_per_block)], axis=0).astype(BF16)
        km_sc[pl.ds(j, 1), :] = jnp.mean(kf, axis=0, keepdims=True)
        s = lax.dot_general(qb, kf.astype(BF16), _NT, preferred_element_type=F32) * scale
        m = jnp.max(s, axis=-1, keepdims=True)
        p = jnp.exp(s - m)
        m_sc[j] = jnp.broadcast_to(m, (rows, LANES))
        l_sc[j] = jnp.broadcast_to(jnp.sum(p, axis=-1, keepdims=True), (rows, LANES))
        o_sc[j] = jnp.dot(p.astype(BF16), vb, preferred_element_type=F32)

    @pl.when(s_id == pl.num_programs(1) - 1)
    def _finish():
        gate = lax.dot_general(qf, km_sc[...], _NT, precision=lax.Precision.HIGHEST, preferred_element_type=F32)
        col = lax.broadcasted_iota(I32, (rows, LANES), 1)
        gate = jnp.where(col < n_past_blocks, gate, NEG_INF)
        sel = _select_topk_lanes(gate, MOBA_TOPK)
        t_of_row = lax.broadcasted_iota(I32, (rows, 8), 0) % n_new
        s_new = lax.dot_general(qb, kn_ref[...].astype(BF16), _NT, preferred_element_type=F32) * scale
        s_new = jnp.where(_new_key_mask(rows, n_new, t_of_row), s_new, NEG_INF)
        m, l, acc = _softmax_step(_softmax_init(rows, KV_W), s_new, vn_ref[...].astype(BF16))
        for j in range(n_past_blocks):
            sel_j = sel[:, j:j + 1]
            mj = jnp.where(sel_j, m_sc[j][:, 0:1], NEG_INF)
            m_new = jnp.maximum(m, mj)
            alpha = jnp.exp(m - m_new)
            beta = jnp.exp(mj - m_new)
            l = alpha * l + beta * l_sc[j][:, 0:1]
            acc = alpha * acc + beta * o_sc[j]
            m = m_new
        o = acc / l
        n_grp = B_HEADS // B_KV_HEADS
        for kvh in range(B_KV_HEADS):
            for g in range(n_grp):
                h = kvh * n_grp + g
                r0 = (kvh * n_grp + g) * n_new
                o_ref[:, h * LANES:(h + 1) * LANES] = o[r0:r0 + n_new, kvh * LANES:(kvh + 1) * LANES]


def _decode_call(kernel, pt_flat, pre_inputs, pre_specs, cache_k, cache_v, scratch, *, bsz, n_new, n_steps, n_pages, name):
    page_specs = _paged_specs(n_pages, n_steps)
    return pl.pallas_call(
        kernel,
        grid_spec=pltpu.PrefetchScalarGridSpec(
            num_scalar_prefetch=1,
            grid=(bsz, n_steps),
            in_specs=pre_specs + page_specs + page_specs,
            out_specs=pl.BlockSpec((None, n_new, Q_W), lambda b, s, pt: (b, 0, 0)),
            scratch_shapes=scratch),
        out_shape=jax.ShapeDtypeStruct((bsz, n_new, Q_W), F32),
        compiler_params=_params(("arbitrary", "arbitrary")),
        name=name,
    )(pt_flat, *pre_inputs, *([cache_k] * n_pages), *([cache_v] * n_pages))


def _pad_new(x3):
    return jnp.pad(x3, ((0, 0), (0, 8 - x3.shape[1]), (0, 0)))


def _diff_attn_decode(qa, ka_new, va_new, cache_k, cache_v, page_table, prm, *, n_pages_per_step):
    bsz, n_new = qa.shape[:2]
    n_grp = A_HEADS // A_KV_HEADS
    q6 = qa.reshape(bsz, n_new, A_KV_HEADS, n_grp, 2, A_DH).transpose(0, 2, 4, 3, 1, 5)
    eye = jnp.eye(A_KV_HEADS * 2, dtype=qa.dtype).reshape(A_KV_HEADS, 2, 1, 1, A_KV_HEADS * 2, 1)
    qbd = (q6[:, :, :, :, :, None, :] * eye[None]).reshape(bsz, A_KV_HEADS * 2 * n_grp * n_new, KV_W).astype(BF16)
    rows = qbd.shape[1]
    n_steps = page_table.shape[1] // n_pages_per_step
    lam_spec = _const_spec((1, A_DH))
    per_b = lambda r, w: pl.BlockSpec((None, r, w), lambda b, s, pt: (b, 0, 0))
    return _decode_call(
        functools.partial(_diff_decode_kernel, n_pages=n_pages_per_step, n_new=n_new),
        page_table.reshape(-1),
        [prm["a_lq1"], prm["a_lk1"], prm["a_lq2"], prm["a_lk2"], qbd, _pad_new(ka_new), _pad_new(va_new), prm["a_subln"]],
        [lam_spec, lam_spec, lam_spec, lam_spec, per_b(rows, KV_W), per_b(8, KV_W), per_b(8, KV_W), _const_spec((1, LANES))],
        cache_k, cache_v,
        [pltpu.VMEM((rows, 1), F32), pltpu.VMEM((rows, 1), F32), pltpu.VMEM((rows, KV_W), F32)],
        bsz=bsz, n_new=n_new, n_steps=n_steps, n_pages=n_pages_per_step, name="diff_attn_decode")


def _moba_decode(qb, kb_new, vb_new, cache_k, cache_v, page_table, *, n_pages_per_step):
    bsz, n_new = qb.shape[:2]
    n_grp = B_HEADS // B_KV_HEADS
    q5 = qb.reshape(bsz, n_new, B_KV_HEADS, n_grp, B_DH).transpose(0, 2, 3, 1, 4)
    eye = jnp.eye(B_KV_HEADS, dtype=qb.dtype).reshape(B_KV_HEADS, 1, 1, B_KV_HEADS, 1)
    qbd = (q5[:, :, :, :, None, :] * eye[None]).reshape(bsz, B_KV_HEADS * n_grp * n_new, KV_W)
    rows = qbd.shape[1]
    n_steps = page_table.shape[1] // n_pages_per_step
    n_past_blocks = page_table.shape[1] * PAGE_SIZE // MOBA_BLOCK
    assert (page_table.shape[1] * PAGE_SIZE) % MOBA_BLOCK == 0 and n_past_blocks <= LANES
    per_b = lambda r, w: pl.BlockSpec((None, r, w), lambda b, s, pt: (b, 0, 0))
    return _decode_call(
        functools.partial(_moba_decode_kernel, n_pages=n_pages_per_step, n_new=n_new, n_past_blocks=n_past_blocks),
        page_table.reshape(-1),
        [qbd, _pad_new(kb_new), _pad_new(vb_new)],
        [per_b(rows, KV_W), per_b(8, KV_W), per_b(8, KV_W)],
        cache_k, cache_v,
        [pltpu.VMEM((n_past_blocks, rows, LANES), F32), pltpu.VMEM((n_past_blocks, rows, LANES), F32),
         pltpu.VMEM((n_past_blocks, rows, KV_W), F32), pltpu.VMEM((LANES, KV_W), F32)],
        bsz=bsz, n_new=n_new, n_steps=n_steps, n_pages=n_pages_per_step, name="moba_decode")


def _topk_rows(s, k, payload=None):
    n_rows = s.shape[0]
    row = lax.broadcasted_iota(I32, s.shape, 0).astype(F32)
    vals, picks = [], []
    for _ in range(k):
        mx = jnp.max(s, axis=0, keepdims=True)
        ix = jnp.min(jnp.where(s == mx, row, float(n_rows)), axis=0, keepdims=True)
        hit = row == ix
        vals.append(mx)
        picks.append(ix if payload is None else jnp.sum(jnp.where(hit, payload, 0.0), axis=0, keepdims=True))
        s = jnp.where(hit, NEG_INF, s)
    return jnp.concatenate(vals, axis=0), jnp.concatenate(picks, axis=0)


def _merge_route_kernel(x_ref, oa_ref, ob_ref, om_ref, gt_ref, wa_ref, wb_ref, wm_ref, wo_ref, fg_ref, wq_ref, keys_ref,
                        x1_ref, h2_ref, eid_ref, gw_ref):
    ya = jnp.dot(oa_ref[...].astype(BF16), wa_ref[...], preferred_element_type=F32)
    yb = jnp.dot(ob_ref[...].astype(BF16), wb_ref[...], preferred_element_type=F32)
    ym = jnp.dot(om_ref[...].astype(BF16), wm_ref[...], preferred_element_type=F32)
    merged = (gt_ref[:, 0:D_MODEL] * ya + gt_ref[:, D_MODEL:2 * D_MODEL] * yb + gt_ref[:, 2 * D_MODEL:3 * D_MODEL] * ym)
    x1 = x_ref[...] + jnp.dot(merged.astype(BF16), wo_ref[...], preferred_element_type=F32)
    x1_ref[...] = x1
    h2 = _rms(x1, fg_ref[...], D_MODEL)
    h2_ref[...] = h2
    qp = jnp.dot(h2.astype(BF16), wq_ref[...], preferred_element_type=F32)
    half = PEER_DQ // 2
    for h in range(PEER_HEADS):
        sub = []
        for p in range(2):
            qhp = qp[:, (h * 2 + p) * half:(h * 2 + p + 1) * half].astype(BF16)
            st = lax.dot_general(keys_ref[h * 2 + p], qhp, _NT, preferred_element_type=F32)
            sub.append(_topk_rows(st, PEER_TOPK))
        (s1, i1), (s2, i2) = sub
        cand_s = jnp.concatenate([s1[a:a + 1, :] + s2 for a in range(PEER_TOPK)], axis=0)
        cand_i = jnp.concatenate([i1[a:a + 1, :] * PEER_NKEYS + i2 for a in range(PEER_TOPK)], axis=0)
        top_s, expert = _topk_rows(cand_s, PEER_TOPK, payload=cand_i)
        e = jnp.exp(top_s - top_s[0:1, :])
        gw_ref[h * PEER_TOPK:(h + 1) * PEER_TOPK, :] = e / jnp.sum(e, axis=0, keepdims=True)
        eid_ref[h * PEER_TOPK:(h + 1) * PEER_TOPK, :] = expert.astype(I32)


def _merge_route(x2d, oa, ob, om, gates, prm, *, tm):
    n = x2d.shape[0]
    row = lambda w: pl.BlockSpec((tm, w), lambda i: (i, 0))
    colblk = pl.BlockSpec((PEER_HEADS * PEER_TOPK, tm), lambda i: (0, i))
    sq = _const_spec((D_MODEL, D_MODEL))
    return pl.pallas_call(
        _merge_route_kernel,
        grid=(n // tm,),
        in_specs=[row(D_MODEL), row(Q_W), row(Q_W), row(Q_W), row(GATE_W), sq, sq, sq, sq, _const_spec((1, D_MODEL)),
                  _const_spec((D_MODEL, PEER_HEADS * PEER_DQ)), _const_spec((PEER_HEADS * 2, PEER_NKEYS, PEER_DQ // 2))],
        out_specs=[row(D_MODEL), row(D_MODEL), colblk, colblk],
        out_shape=[jax.ShapeDtypeStruct((n, D_MODEL), F32), jax.ShapeDtypeStruct((n, D_MODEL), F32),
                   jax.ShapeDtypeStruct((PEER_HEADS * PEER_TOPK, n), I32), jax.ShapeDtypeStruct((PEER_HEADS * PEER_TOPK, n), F32)],
        compiler_params=_params(("arbitrary",)),
        name="merge_route",
    )(x2d, oa, ob, om, gates, prm["w_br_a"], prm["w_br_b"], prm["w_br_m"], prm["w_out"], prm["ffn_norm_g"],
      prm["w_peer_q"], prm["peer_keys"])


def _pack_table(tab):
    bits = lax.bitcast_convert_type(tab.astype(jnp.bfloat16), jnp.uint16).astype(jnp.uint32)
    half = tab.shape[1] // 2
    words = bits[:, :half] | (bits[:, half:] << 16)
    return lax.bitcast_convert_type(words, I32).reshape(tab.shape[0] * PEER_SLAB, LANES)


def _unpack_row(tab_ref, e):
    slab = tab_ref[pl.ds(pl.multiple_of(e * PEER_SLAB, PEER_SLAB), PEER_SLAB), :]
    lo = lax.bitcast_convert_type(slab << 16, F32)
    hi = lax.bitcast_convert_type(slab & jnp.int32(-65536), F32)
    return lo, hi


def _peer_act_kernel(eid_ref, x_ref, gw_ref, fold_ref, tab_ref, w_ref, s_sc, z_sc, *, tt):
    n_k = PEER_HEADS * PEER_TOPK

    def token(t, carry):
        x8 = x_ref[t]
        xlo, xhi = x8[0:PEER_SLAB], x8[PEER_SLAB:2 * PEER_SLAB]
        for k in range(n_k):
            lo, hi = _unpack_row(tab_ref, eid_ref[t, k])
            s_sc[k * PEER_SLAB:(k + 1) * PEER_SLAB, :] = lo * xlo + hi * xhi
        z_sc[pl.ds(t, 1), :] = jnp.sum(s_sc[...].T, axis=0, keepdims=True)
        return carry

    lax.fori_loop(0, tt, token, 0)
    z = z_sc[...]
    z_hi = z.astype(BF16)
    z_lo = (z - z_hi.astype(F32)).astype(BF16)
    a = (jnp.dot(z_hi, fold_ref[...], preferred_element_type=F32) + jnp.dot(z_lo, fold_ref[...], preferred_element_type=F32))
    gelu = 0.5 * a * (1.0 + lax.erf(a * (2.0 ** -0.5)))
    w_ref[...] = gw_ref[...] * gelu


def _peer_out_kernel(eid_ref, w_ref, x1_ref, tab_ref, y_ref, *, tt):
    n_k = PEER_HEADS * PEER_TOPK

    def token(t, carry):
        acc = [[jnp.zeros((PEER_SLAB, LANES), F32), jnp.zeros((PEER_SLAB, LANES), F32)] for _ in range(2)]
        for k in range(n_k):
            lo, hi = _unpack_row(tab_ref, eid_ref[t, k])
            w = w_ref[t, k]
            acc[k % 2][0] = acc[k % 2][0] + w * lo
            acc[k % 2][1] = acc[k % 2][1] + w * hi
        y_ref[t] = x1_ref[t] + jnp.concatenate([acc[0][0] + acc[1][0], acc[0][1] + acc[1][1]], axis=0)
        return carry

    lax.fori_loop(0, tt, token, 0)


def _peer(h2, x1, eid, gw, u_pack, v_pack, *, tt):
    n = h2.shape[0]
    n_k = PEER_HEADS * PEER_TOPK
    rows8 = D_MODEL // LANES
    smem = lambda: pl.BlockSpec((tt, n_k), lambda i: (i, 0), memory_space=pltpu.SMEM)
    tok3 = pl.BlockSpec((tt, rows8, LANES), lambda i: (i, 0, 0))
    tab_spec = _const_spec((PEER_N * PEER_SLAB, LANES), True)
    fold = (jnp.arange(n_k * PEER_SLAB)[:, None] // PEER_SLAB == jnp.arange(n_k)[None, :]).astype(BF16)
    w = pl.pallas_call(
        functools.partial(_peer_act_kernel, tt=tt),
        grid=(n // tt,),
        in_specs=[smem(), tok3, pl.BlockSpec((tt, n_k), lambda i: (i, 0)), _const_spec((n_k * PEER_SLAB, n_k)), tab_spec],
        out_specs=pl.BlockSpec((tt, n_k), lambda i: (i, 0)),
        out_shape=jax.ShapeDtypeStruct((n, n_k), F32),
        scratch_shapes=[pltpu.VMEM((n_k * PEER_SLAB, LANES), F32), pltpu.VMEM((tt, n_k * PEER_SLAB), F32)],
        compiler_params=_params(("arbitrary",)),
        name="peer_act",
    )(eid, h2.reshape(n, rows8, LANES), gw, fold, u_pack)
    y = pl.pallas_call(
        functools.partial(_peer_out_kernel, tt=tt),
        grid=(n // tt,),
        in_specs=[smem(), smem(), tok3, tab_spec],
        out_specs=tok3,
        out_shape=jax.ShapeDtypeStruct((n, rows8, LANES), F32),
        compiler_params=_params(("arbitrary",)),
        name="peer_out",
    )(eid, w, x1.reshape(n, rows8, LANES), v_pack)
    return y.reshape(n, D_MODEL)


def _merge_ffn(x2d, oa, ob, om, gates, prm, u_pack, v_pack, *, tm, tt):
    x1, h2, eid_t, gw_t = _merge_route(x2d, oa, ob, om, gates, prm, tm=tm)
    return _peer(h2, x1, eid_t.T, gw_t.T, u_pack, v_pack, tt=tt)


def kernel(x_prompt, x_sample, mem_prompt, cache_a_k, cache_a_v, cache_b_k, cache_b_v, cache_mem_k, cache_mem_v, page_table, attn_norm_g, w_in, a_q_norm, a_k_norm, a_lq1, a_lk1, a_lq2, a_lk2, a_subln, b_q_norm, b_k_norm, mem_norm_g, w_mem_kv, m_q_norm, m_k_norm, w_br_a, w_br_b, w_br_m, w_out, ffn_norm_g, w_peer_q, peer_keys, peer_u, peer_v):
    assert attn_norm_g.shape[0] == 1, "single trunk layer"
    bsz, seq, _ = x_prompt.shape
    dbsz, n_new, _ = x_sample.shape
    n_mem = mem_prompt.shape[1]
    n_pool = cache_a_k.shape[1]
    past = page_table.shape[1] * PAGE_SIZE

    prm = dict(
        attn_norm_g=attn_norm_g, w_in=w_in[0].astype(BF16),
        a_q_norm=jnp.tile(a_q_norm, (1, 2)), a_k_norm=jnp.tile(a_k_norm, (1, 2)),
        a_lq1=a_lq1, a_lk1=a_lk1, a_lq2=a_lq2, a_lk2=a_lk2, a_subln=a_subln,
        b_q_norm=b_q_norm, b_k_norm=b_k_norm, mem_norm_g=mem_norm_g, w_mem_kv=w_mem_kv[0].astype(BF16),
        m_q_norm=m_q_norm, m_k_norm=m_k_norm,
        w_br_a=w_br_a[0].astype(BF16), w_br_b=w_br_b[0].astype(BF16), w_br_m=w_br_m[0].astype(BF16),
        w_out=w_out[0].astype(BF16), ffn_norm_g=ffn_norm_g, w_peer_q=w_peer_q[0].astype(BF16),
        peer_keys=peer_keys[0].reshape(PEER_HEADS * 2, PEER_NKEYS, PEER_DQ // 2).astype(BF16))
    u_pack = _pack_table(peer_u[0])
    v_pack = _pack_table(peer_v[0])

    tm = 256
    pos_p = jnp.arange(seq, dtype=I32)
    xp2 = x_prompt.reshape(bsz * seq, D_MODEL)
    qa, ka, va, qb, kb, vb, qm, gates = _in_proj(xp2, _rope_table(pos_p, A_DH), _rope_table(pos_p, B_DH), prm, tm=tm, q_dtype=BF16)
    r3 = lambda a: a.reshape(bsz, seq, a.shape[-1])
    oa = _diff_attn_prompt(r3(qa), r3(ka), r3(va), prm, tq=256)
    ob = _moba_prompt(r3(qb), r3(kb), r3(vb))
    mk, mv = _mem_kv(mem_prompt.reshape(bsz * n_mem, D_MODEL), prm, tm=tm)
    om = _mem_attn(r3(qm), mk.reshape(bsz, n_mem, Q_W), mv.reshape(bsz, n_mem, Q_W), tq=256, out_dtype=BF16)
    y_prompt = _merge_ffn(xp2, oa.reshape(-1, Q_W), ob.reshape(-1, Q_W), om.reshape(-1, Q_W), gates, prm, u_pack, v_pack, tm=tm, tt=32)

    pos_s = jnp.tile(past + jnp.arange(n_new, dtype=I32), tm // n_new)
    xs2 = x_sample.reshape(dbsz * n_new, D_MODEL)
    sqa, ska, sva, sqb, skb, svb, sqm, sgates = _in_proj(xs2, _rope_table(pos_s, A_DH), _rope_table(pos_s, B_DH), prm, tm=tm, q_dtype=F32)
    s3 = lambda a: a.reshape(dbsz, n_new, a.shape[-1])
    pool3 = lambda c: c[0].reshape(n_pool, PAGE_SIZE, KV_W)
    soa = _diff_attn_decode(s3(sqa), s3(ska), s3(sva), pool3(cache_a_k), pool3(cache_a_v), page_table, prm, n_pages_per_step=8)
    sob = _moba_decode(s3(sqb), s3(skb), s3(svb), pool3(cache_b_k), pool3(cache_b_v), page_table, n_pages_per_step=8)
    som = _mem_attn(s3(sqm), cache_mem_k[0].reshape(dbsz, n_mem, Q_W), cache_mem_v[0].reshape(dbsz, n_mem, Q_W), tq=n_new, out_dtype=F32)
    y_sample = _merge_ffn(xs2, soa.reshape(-1, Q_W), sob.reshape(-1, Q_W), som.reshape(-1, Q_W), sgates, prm, u_pack, v_pack, tm=tm, tt=32)

    return (y_prompt.reshape(bsz, seq, D_MODEL), y_sample.reshape(dbsz, n_new, D_MODEL),
            ka.reshape(1, bsz, seq, A_KV_HEADS, 2, A_DH), va.reshape(1, bsz, seq, A_KV_HEADS, 2 * A_DH),
            kb.reshape(1, bsz, seq, B_KV_HEADS, B_DH), vb.reshape(1, bsz, seq, B_KV_HEADS, B_DH),
            mk.reshape(1, bsz, n_mem, M_HEADS, M_DH), mv.reshape(1, bsz, n_mem, M_HEADS, M_DH),
            ska.reshape(1, dbsz, n_new, A_KV_HEADS, 2, A_DH), sva.reshape(1, dbsz, n_new, A_KV_HEADS, 2 * A_DH),
            skb.reshape(1, dbsz, n_new, B_KV_HEADS, B_DH), svb.reshape(1, dbsz, n_new, B_KV_HEADS, B_DH))
```

```python
import functools
import math

import jax
import jax.numpy as jnp
from jax import lax
from jax.experimental import pallas as pl
from jax.experimental.pallas import tpu as pltpu

F32 = jnp.float32
BF16 = jnp.bfloat16
I32 = jnp.int32
NEG_INF = float("-inf")

D_MODEL = 1024
PAGE_SIZE = 128
A_HEADS, A_KV_HEADS, A_DH = 8, 4, 64
B_HEADS, B_KV_HEADS, B_DH = 8, 4, 128
MOBA_BLOCK, MOBA_TOPK = 256, 3
M_HEADS, M_DH = 4, 256
ROPE_THETA, ROT_DIV = 500000.0, 4
PEER_HEADS, PEER_NKEYS, PEER_DQ, PEER_TOPK = 8, 128, 256, 16
PEER_N = PEER_NKEYS * PEER_NKEYS
EPS = 1e-6
LAM_INIT = 0.8 - 0.6 * math.exp(-0.3 * 0)

LANES = 128
KV_W = 512
PAGE_ROWS = PAGE_SIZE * KV_W // LANES
Q_W = 1024
GATE_W = 3 * D_MODEL
IN_W = 3 * Q_W + 4 * KV_W + GATE_W
PEER_SLAB = 4
PEER_XPOSE_UNROLL = 4
VMEM_LIMIT = 56 * 1024 * 1024

_NT = (((1,), (1,)), ((), ()))


def _const_spec(shape, single_buffer=False):
    index_map = lambda *_: (0,) * len(shape)
    if single_buffer:
        return pl.BlockSpec(shape, index_map, pipeline_mode=pl.Buffered(1))
    return pl.BlockSpec(shape, index_map)


def _params(sem):
    return pltpu.CompilerParams(dimension_semantics=sem, vmem_limit_bytes=VMEM_LIMIT)


def _rms(x, g, width):
    return x * lax.rsqrt(jnp.sum(x * x, axis=-1, keepdims=True) * (1.0 / width) + EPS) * g


def _rope_table(pos, seg):
    r = seg // ROT_DIV
    half = r // 2
    inv = ROPE_THETA ** (-jnp.arange(half, dtype=F32) * (2.0 / r))
    ang = pos.astype(F32)[:, None] * inv[None, :]
    cos, sin = jnp.cos(ang), jnp.sin(ang)
    n = pos.shape[0]
    z_half = jnp.zeros((n, half), F32)
    z_rest = jnp.zeros((n, seg - r), F32)
    c = jnp.concatenate([cos, cos, jnp.ones((n, seg - r), F32)], axis=1)
    sm = jnp.concatenate([-sin, z_half, z_rest], axis=1)
    sp = jnp.concatenate([z_half, sin, z_rest], axis=1)
    reps = LANES // seg
    return jnp.concatenate([jnp.tile(c, (1, reps)), jnp.tile(sm, (1, reps)), jnp.tile(sp, (1, reps))], axis=1)


def _rope(y, tab_ref, half):
    c = tab_ref[:, 0:LANES]
    sm = tab_ref[:, LANES:2 * LANES]
    sp = tab_ref[:, 2 * LANES:3 * LANES]
    return y * c + pltpu.roll(y, LANES - half, 1) * sm + pltpu.roll(y, half, 1) * sp


def _in_proj_kernel(x_ref, g_ref, w_ref, ropea_ref, ropeb_ref, aqn_ref, akn_ref, bqn_ref, bkn_ref, mqn_ref,
                    qa_ref, ka_ref, va_ref, qb_ref, kb_ref, vb_ref, qm_ref, gt_ref):
    x = x_ref[...]
    hb = _rms(x, g_ref[...], D_MODEL).astype(BF16)
    tm = x.shape[0]
    lane = lax.broadcasted_iota(I32, (tm, LANES), 1)
    lo = lane < A_DH

    def mm(c0, width):
        return jnp.dot(hb, w_ref[:, c0:c0 + width], preferred_element_type=F32)

    def norm64(y, gn):
        ss = y * y
        s_lo = jnp.sum(jnp.where(lo, ss, 0.0), axis=-1, keepdims=True)
        s_hi = jnp.sum(jnp.where(lo, 0.0, ss), axis=-1, keepdims=True)
        inv = jnp.where(lo, lax.rsqrt(s_lo * (1.0 / A_DH) + EPS), lax.rsqrt(s_hi * (1.0 / A_DH) + EPS))
        return y * inv * gn

    c0 = 0
    y = mm(c0, Q_W)
    for b in range(Q_W // LANES):
        blk = slice(b * LANES, (b + 1) * LANES)
        qa_ref[:, blk] = (_rope(norm64(y[:, blk], aqn_ref[...]), ropea_ref, A_DH // ROT_DIV // 2)
                          * (A_DH ** -0.5)).astype(qa_ref.dtype)
    c0 += Q_W
    y = mm(c0, KV_W)
    for b in range(KV_W // LANES):
        blk = slice(b * LANES, (b + 1) * LANES)
        ka_ref[:, blk] = _rope(norm64(y[:, blk], akn_ref[...]), ropea_ref, A_DH // ROT_DIV // 2)
    c0 += KV_W
    va_ref[...] = mm(c0, KV_W)
    c0 += KV_W
    y = mm(c0, Q_W)
    for b in range(Q_W // LANES):
        blk = slice(b * LANES, (b + 1) * LANES)
        qb_ref[:, blk] = _rope(_rms(y[:, blk], bqn_ref[...], B_DH), ropeb_ref, B_DH // ROT_DIV // 2)
    c0 += Q_W
    y = mm(c0, KV_W)
    for b in range(KV_W // LANES):
        blk = slice(b * LANES, (b + 1) * LANES)
        kb_ref[:, blk] = _rope(_rms(y[:, blk], bkn_ref[...], B_DH), ropeb_ref, B_DH // ROT_DIV // 2)
    c0 += KV_W
    vb_ref[...] = mm(c0, KV_W)
    c0 += KV_W
    y = mm(c0, Q_W)
    for h in range(M_HEADS):
        blk = slice(h * M_DH, (h + 1) * M_DH)
        qm_ref[:, blk] = (_rms(y[:, blk], mqn_ref[...], M_DH) * (M_DH ** -0.5)).astype(qm_ref.dtype)
    c0 += Q_W
    for b in range(3):
        gt_ref[:, b * D_MODEL:(b + 1) * D_MODEL] = jax.nn.sigmoid(mm(c0 + b * D_MODEL, D_MODEL))


def _in_proj(x2d, rope_a, rope_b, prm, *, tm, q_dtype):
    n = x2d.shape[0]
    n_rope = rope_a.shape[0] // tm
    row = lambda w: pl.BlockSpec((tm, w), lambda i: (i, 0))
    rope_spec = pl.BlockSpec((tm, 3 * LANES), lambda i: (i % n_rope, 0))
    out_shape = [jax.ShapeDtypeStruct((n, Q_W), q_dtype), jax.ShapeDtypeStruct((n, KV_W), F32),
                 jax.ShapeDtypeStruct((n, KV_W), F32), jax.ShapeDtypeStruct((n, Q_W), F32),
                 jax.ShapeDtypeStruct((n, KV_W), F32), jax.ShapeDtypeStruct((n, KV_W), F32),
                 jax.ShapeDtypeStruct((n, Q_W), q_dtype), jax.ShapeDtypeStruct((n, GATE_W), F32)]
    return pl.pallas_call(
        _in_proj_kernel,
        grid=(n // tm,),
        in_specs=[row(D_MODEL), _const_spec((1, D_MODEL)), _const_spec((D_MODEL, IN_W), True), rope_spec, rope_spec,
                  _const_spec((1, LANES)), _const_spec((1, LANES)), _const_spec((1, LANES)), _const_spec((1, LANES)),
                  _const_spec((1, M_DH))],
        out_specs=[row(Q_W), row(KV_W), row(KV_W), row(Q_W), row(KV_W), row(KV_W), row(Q_W), row(GATE_W)],
        out_shape=out_shape,
        compiler_params=_params(("arbitrary",)),
        name="in_proj",
    )(x2d, prm["attn_norm_g"], prm["w_in"], rope_a, rope_b, prm["a_q_norm"], prm["a_k_norm"],
      prm["b_q_norm"], prm["b_k_norm"], prm["m_q_norm"])


def _diff_lambda(lq1_ref, lk1_ref, lq2_ref, lk2_ref):
    return (jnp.exp(jnp.sum(lq1_ref[...] * lk1_ref[...], keepdims=True))
            - jnp.exp(jnp.sum(lq2_ref[...] * lk2_ref[...], keepdims=True)) + LAM_INIT)


def _softmax_step(carry, s, v):
    m, l, acc = carry
    m_new = jnp.maximum(m, jnp.max(s, axis=-1, keepdims=True))
    alpha = jnp.exp(m - m_new)
    p = jnp.exp(s - m_new)
    l = alpha * l + jnp.sum(p, axis=-1, keepdims=True)
    pv = v(p.astype(BF16)) if callable(v) else jnp.dot(p.astype(BF16), v, preferred_element_type=F32)
    return m_new, l, alpha * acc + pv


def _softmax_init(rows, width):
    return (jnp.full((rows, 1), NEG_INF, F32), jnp.zeros((rows, 1), F32), jnp.zeros((rows, width), F32))


def _diff_attn_kernel(lq1_ref, lk1_ref, lq2_ref, lk2_ref, q_ref, k_ref, v_ref, subln_ref, o_ref, kb_sc, vb_sc, *, tq):
    qi = pl.program_id(2)

    @pl.when(qi == 0)
    def _cast_kv():
        kb_sc[...] = k_ref[...].astype(BF16)
        vb_sc[...] = v_ref[...].astype(BF16)

    lam = _diff_lambda(lq1_ref, lk1_ref, lq2_ref, lk2_ref)
    lane = lax.broadcasted_iota(I32, (tq, LANES), 1)
    r = lax.broadcasted_iota(I32, (2 * tq, tq), 0)
    r = jnp.where(r >= tq, r - tq, r)
    causal = r >= lax.broadcasted_iota(I32, (2 * tq, tq), 1)

    for g in range(A_HEADS // A_KV_HEADS):
        qg = q_ref[:, g * LANES:(g + 1) * LANES]
        zero = jnp.zeros_like(qg)
        qq = jnp.concatenate([jnp.where(lane < A_DH, qg, zero), jnp.where(lane >= A_DH, qg, zero)], axis=0)

        def step(j, carry, masked):
            kj = kb_sc[pl.ds(pl.multiple_of(j * tq, tq), tq), :]
            vj = vb_sc[pl.ds(pl.multiple_of(j * tq, tq), tq), :]
            s = lax.dot_general(qq, kj, _NT, preferred_element_type=F32)
            if masked:
                s = jnp.where(causal, s, NEG_INF)
            return _softmax_step(carry, s, vj)

        carry = step(qi, _softmax_init(2 * tq, LANES), True)
        m, l, acc = lax.fori_loop(0, qi, lambda j, c: step(j, c, False), carry)
        o = acc / l
        o = o[:tq] - lam * o[tq:]
        o = _rms(o, subln_ref[...], 2 * A_DH) * (1.0 - LAM_INIT)
        o_ref[:, g * LANES:(g + 1) * LANES] = o.astype(o_ref.dtype)


def _diff_attn_prompt(qa, ka, va, prm, *, tq):
    bsz, seq = qa.shape[:2]
    grp = (A_HEADS // A_KV_HEADS) * LANES
    lam_spec = _const_spec((1, A_DH))
    return pl.pallas_call(
        functools.partial(_diff_attn_kernel, tq=tq),
        grid=(bsz, A_KV_HEADS, seq // tq),
        in_specs=[lam_spec, lam_spec, lam_spec, lam_spec,
                  pl.BlockSpec((None, tq, grp), lambda b, h, i: (b, i, h)),
                  pl.BlockSpec((None, seq, LANES), lambda b, h, i: (b, 0, h)),
                  pl.BlockSpec((None, seq, LANES), lambda b, h, i: (b, 0, h)),
                  _const_spec((1, LANES))],
        out_specs=pl.BlockSpec((None, tq, grp), lambda b, h, i: (b, i, h)),
        out_shape=jax.ShapeDtypeStruct((bsz, seq, Q_W), BF16),
        scratch_shapes=[pltpu.VMEM((seq, LANES), BF16), pltpu.VMEM((seq, LANES), BF16)],
        compiler_params=_params(("arbitrary", "arbitrary", "arbitrary")),
        name="diff_attn_prompt",
    )(prm["a_lq1"], prm["a_lk1"], prm["a_lq2"], prm["a_lk2"], qa, ka, va, prm["a_subln"])


def _select_topk_lanes(gate, k):
    col = lax.broadcasted_iota(I32, gate.shape, 1).astype(F32)
    sel = jnp.zeros(gate.shape, jnp.bool_)
    for _ in range(k):
        mx = jnp.max(gate, axis=-1, keepdims=True)
        ix = jnp.min(jnp.where(gate == mx, col, float(LANES)), axis=-1, keepdims=True)
        hit = col == ix
        sel = sel | (hit & (mx > NEG_INF))
        gate = jnp.where(hit, NEG_INF, gate)
    return sel


def _moba_kernel(q_ref, k_ref, v_ref, o_ref, kb_sc, vb_sc, km_sc, *, tq, n_blocks):
    qi = pl.program_id(2)

    @pl.when(qi == 0)
    def _prep_kv():
        kb_sc[...] = k_ref[...].astype(BF16)
        vb_sc[...] = v_ref[...].astype(BF16)
        km_sc[...] = jnp.zeros_like(km_sc)
        for j in range(n_blocks):
            km_sc[j:j + 1, :] = jnp.mean(k_ref[j * MOBA_BLOCK:(j + 1) * MOBA_BLOCK, :], axis=0, keepdims=True)

    n_grp = B_HEADS // B_KV_HEADS
    qf = jnp.concatenate([q_ref[:, g * LANES:(g + 1) * LANES] for g in range(n_grp)], axis=0)
    rows = n_grp * tq
    gate = lax.dot_general(qf, km_sc[...], _NT, precision=lax.Precision.HIGHEST, preferred_element_type=F32)
    col = lax.broadcasted_iota(I32, (rows, LANES), 1)
    gate = jnp.where(col < qi, gate, NEG_INF)
    sel = _select_topk_lanes(gate, MOBA_TOPK).astype(F32)

    qb = qf.astype(BF16)
    r = lax.broadcasted_iota(I32, (rows, tq), 0)
    for g in range(1, n_grp):
        r = jnp.where(r >= g * tq, r - tq, r)
    causal = r >= lax.broadcasted_iota(I32, (rows, tq), 1)
    scale = B_DH ** -0.5

    def step(j, carry, own):
        kj = kb_sc[pl.ds(pl.multiple_of(j * tq, tq), tq), :]
        vj = vb_sc[pl.ds(pl.multiple_of(j * tq, tq), tq), :]
        s = lax.dot_general(qb, kj, _NT, preferred_element_type=F32) * scale
        if own:
            s = jnp.where(causal, s, NEG_INF)
        else:
            sel_j = jnp.sum(jnp.where(col == j, sel, 0.0), axis=-1, keepdims=True) > 0.0
            s = jnp.where(sel_j, s, NEG_INF)
        return _softmax_step(carry, s, vj)

    carry = step(qi, _softmax_init(rows, LANES), True)
    m, l, acc = lax.fori_loop(0, qi, lambda j, c: step(j, c, False), carry)
    o = acc / l
    for g in range(n_grp):
        o_ref[:, g * LANES:(g + 1) * LANES] = o[g * tq:(g + 1) * tq].astype(o_ref.dtype)


def _moba_prompt(qb, kb, vb):
    bsz, seq = qb.shape[:2]
    tq = MOBA_BLOCK
    n_blocks = seq // MOBA_BLOCK
    assert seq % MOBA_BLOCK == 0 and n_blocks <= LANES
    grp = (B_HEADS // B_KV_HEADS) * LANES
    return pl.pallas_call(
        functools.partial(_moba_kernel, tq=tq, n_blocks=n_blocks),
        grid=(bsz, B_KV_HEADS, seq // tq),
        in_specs=[pl.BlockSpec((None, tq, grp), lambda b, h, i: (b, i, h)),
                  pl.BlockSpec((None, seq, LANES), lambda b, h, i: (b, 0, h)),
                  pl.BlockSpec((None, seq, LANES), lambda b, h, i: (b, 0, h))],
        out_specs=pl.BlockSpec((None, tq, grp), lambda b, h, i: (b, i, h)),
        out_shape=jax.ShapeDtypeStruct((bsz, seq, Q_W), BF16),
        scratch_shapes=[pltpu.VMEM((seq, LANES), BF16), pltpu.VMEM((seq, LANES), BF16), pltpu.VMEM((LANES, LANES), F32)],
        compiler_params=_params(("arbitrary", "arbitrary", "arbitrary")),
        name="moba_prompt",
    )(qb, kb, vb)


def _mem_kv_kernel(x_ref, g_ref, w_ref, kn_ref, k_ref, v_ref):
    hb = _rms(x_ref[...], g_ref[...], D_MODEL).astype(BF16)
    y = jnp.dot(hb, w_ref[:, 0:Q_W], preferred_element_type=F32)
    for h in range(M_HEADS):
        blk = slice(h * M_DH, (h + 1) * M_DH)
        k_ref[:, blk] = _rms(y[:, blk], kn_ref[...], M_DH)
    v_ref[...] = jnp.dot(hb, w_ref[:, Q_W:2 * Q_W], preferred_element_type=F32)


def _mem_kv(mem2d, prm, *, tm):
    n = mem2d.shape[0]
    row = pl.BlockSpec((tm, D_MODEL), lambda i: (i, 0))
    return pl.pallas_call(
        _mem_kv_kernel,
        grid=(n // tm,),
        in_specs=[row, _const_spec((1, D_MODEL)), _const_spec((D_MODEL, 2 * Q_W)), _const_spec((1, M_DH))],
        out_specs=[row, row],
        out_shape=[jax.ShapeDtypeStruct((n, Q_W), F32), jax.ShapeDtypeStruct((n, Q_W), F32)],
        compiler_params=_params(("arbitrary",)),
        name="mem_kv",
    )(mem2d, prm["mem_norm_g"], prm["w_mem_kv"], prm["m_k_norm"])


def _mem_attn_kernel(q_ref, k_ref, v_ref, o_ref):
    for h in range(M_HEADS):
        blk = slice(h * M_DH, (h + 1) * M_DH)
        q = q_ref[:, blk].astype(BF16)
        s = lax.dot_general(q, k_ref[:, blk].astype(BF16), _NT, preferred_element_type=F32)
        p = jnp.exp(s - jnp.max(s, axis=-1, keepdims=True))
        p = p / jnp.sum(p, axis=-1, keepdims=True)
        o_ref[:, blk] = jnp.dot(p.astype(BF16), v_ref[:, blk].astype(BF16), preferred_element_type=F32).astype(o_ref.dtype)


def _mem_attn(qm, mk, mv, *, tq, out_dtype):
    bsz, seq = qm.shape[:2]
    n_mem = mk.shape[1]
    return pl.pallas_call(
        _mem_attn_kernel,
        grid=(bsz, seq // tq),
        in_specs=[pl.BlockSpec((None, tq, Q_W), lambda b, i: (b, i, 0)),
                  pl.BlockSpec((None, n_mem, Q_W), lambda b, i: (b, 0, 0)),
                  pl.BlockSpec((None, n_mem, Q_W), lambda b, i: (b, 0, 0))],
        out_specs=pl.BlockSpec((None, tq, Q_W), lambda b, i: (b, i, 0)),
        out_shape=jax.ShapeDtypeStruct((bsz, seq, Q_W), out_dtype),
        compiler_params=_params(("arbitrary", "arbitrary")),
        name="mem_attn",
    )(qm, mk, mv)


def _paged_specs(n_page_ops, n_steps):
    def spec(p):
        return pl.BlockSpec((None, PAGE_ROWS, LANES), lambda b, s, pt: (pt[b * (n_steps * n_page_ops) + s * n_page_ops + p], 0, 0))
    return [spec(p) for p in range(n_page_ops)]


def _token_major_page(cache):
    return cache[0].reshape(cache.shape[1], PAGE_ROWS, LANES)


def _head_rows(ref, h, n_heads):
    return ref[pl.ds(h, PAGE_SIZE, stride=n_heads), :]


def _new_key_mask(rows, n_new, t_of_row):
    col = lax.broadcasted_iota(I32, (rows, 8), 1)
    return (col <= t_of_row) & (col < n_new)


def _diff_decode_kernel(pt_ref, lq1_ref, lk1_ref, lq2_ref, lk2_ref, q_ref, kn_ref, vn_ref, subln_ref, *rest, n_pages, n_new):
    kt_refs, v_refs = rest[:n_pages], rest[n_pages:2 * n_pages]
    o_ref, m_sc, l_sc, acc_sc = rest[2 * n_pages:]
    s_id = pl.program_id(1)
    q = q_ref[...]
    rows = q.shape[0]
    rows_per_kvh = rows // A_KV_HEADS

    @pl.when(s_id == 0)
    def _init():
        m_sc[...] = jnp.full(m_sc.shape, NEG_INF, F32)
        l_sc[...] = jnp.zeros(l_sc.shape, F32)
        acc_sc[...] = jnp.zeros(acc_sc.shape, F32)

    def by_kv_head(p, value_of_head):
        return jnp.concatenate([jnp.dot(p[h * rows_per_kvh:(h + 1) * rows_per_kvh], value_of_head(h), preferred_element_type=F32)
                                for h in range(A_KV_HEADS)], axis=0)

    def paged_values(h):
        return jnp.concatenate([_head_rows(r, h, A_KV_HEADS) for r in v_refs], axis=0).astype(BF16)

    s = jnp.concatenate([jnp.dot(q, r[...].astype(BF16), preferred_element_type=F32) for r in kt_refs], axis=1)
    m, l, acc = _softmax_step((m_sc[...], l_sc[...], acc_sc[...]), s, lambda p: by_kv_head(p, paged_values))
    m_sc[...] = m
    l_sc[...] = l
    acc_sc[...] = acc

    @pl.when(s_id == pl.num_programs(1) - 1)
    def _finish():
        lam = _diff_lambda(lq1_ref, lk1_ref, lq2_ref, lk2_ref)
        t_of_row = lax.broadcasted_iota(I32, (rows, 8), 0) % n_new
        s_new = lax.dot_general(q, kn_ref[...].astype(BF16), _NT, preferred_element_type=F32)
        s_new = jnp.where(_new_key_mask(rows, n_new, t_of_row), s_new, NEG_INF)
        new_values = lambda h: vn_ref[:, h * LANES:(h + 1) * LANES].astype(BF16)
        m2, l2, acc2 = _softmax_step((m_sc[...], l_sc[...], acc_sc[...]), s_new, lambda p: by_kv_head(p, new_values))
        o = acc2 / l2
        n_grp = A_HEADS // A_KV_HEADS
        per_map = n_grp * n_new
        for kvh in range(A_KV_HEADS):
            base = kvh * 2 * per_map
            od = o[base:base + per_map] - lam * o[base + per_map:base + 2 * per_map]
            od = _rms(od, subln_ref[...], 2 * A_DH) * (1.0 - LAM_INIT)
            for g in range(n_grp):
                h = kvh * n_grp + g
                o_ref[:, h * LANES:(h + 1) * LANES] = od[g * n_new:(g + 1) * n_new]


def _moba_decode_kernel(pt_ref, q_ref, kn_ref, vn_ref, *rest, n_pages, n_new, n_past_blocks):
    k_refs, v_refs = rest[:n_pages], rest[n_pages:2 * n_pages]
    o_ref, m_sc, l_sc, o_sc, km_sc = rest[2 * n_pages:]
    s_id = pl.program_id(1)
    qf = q_ref[...]
    qb = qf.astype(BF16)
    rows = qf.shape[0]
    rows_per_kvh = rows // B_KV_HEADS
    scale = B_DH ** -0.5
    pages_per_block = MOBA_BLOCK // PAGE_SIZE
    blocks_per_step = n_pages // pages_per_block
    block_rows = MOBA_BLOCK * B_KV_HEADS
    same_head = (lax.broadcasted_iota(I32, (rows, block_rows), 1) % B_KV_HEADS
                 == lax.broadcasted_iota(I32, (rows, block_rows), 0) // rows_per_kvh)
    head_rows = lambda a, h: a[h * rows_per_kvh:(h + 1) * rows_per_kvh]

    @pl.when(s_id == 0)
    def _init():
        km_sc[...] = jnp.zeros_like(km_sc)

    for jj in range(blocks_per_step):
        j = s_id * blocks_per_step + jj
        kf = jnp.concatenate([k_refs[jj * pages_per_block + p][...] for p in range(pages_per_block)], axis=0)
        vb = jnp.concatenate([v_refs[jj * pages_per_block + p][...] for p in range(pages_per_block)], axis=0).astype(BF16)
        ksum = jnp.sum(kf.reshape(block_rows // 8, 8, LANES), axis=0)
        kmean = (ksum[0:B_KV_HEADS] + ksum[B_KV_HEADS:2 * B_KV_HEADS]) * (1.0 / MOBA_BLOCK)
        for h in range(B_KV_HEADS):
            km_sc[h, pl.ds(j, 1), :] = kmean[h:h + 1]
        s = lax.dot_general(qb, kf.astype(BF16), _NT, preferred_element_type=F32) * scale
        s = jnp.where(same_head, s, NEG_INF)
        m = jnp.max(s, axis=-1, keepdims=True)
        p = jnp.exp(s - m)
        m_sc[j] = jnp.broadcast_to(m, (rows, LANES))
        l_sc[j] = jnp.broadcast_to(jnp.sum(p, axis=-1, keepdims=True), (rows, LANES))
        o_sc[j] = jnp.dot(p.astype(BF16), vb, preferred_element_type=F32)

    @pl.when(s_id == pl.num_programs(1) - 1)
    def _finish():
        gate = jnp.concatenate([lax.dot_general(head_rows(qf, h), km_sc[h], _NT, precision=lax.Precision.HIGHEST,
                                                preferred_element_type=F32) for h in range(B_KV_HEADS)], axis=0)
        col = lax.broadcasted_iota(I32, (rows, LANES), 1)
        gate = jnp.where(col < n_past_blocks, gate, NEG_INF)
        sel = _select_topk_lanes(gate, MOBA_TOPK)
        t_of_row = lax.broadcasted_iota(I32, (rows, 8), 0) % n_new
        new_of_head = lambda ref, h: ref[:, h * LANES:(h + 1) * LANES].astype(BF16)
        s_new = jnp.concatenate([lax.dot_general(head_rows(qb, h), new_of_head(kn_ref, h), _NT, preferred_element_type=F32)
                                 for h in range(B_KV_HEADS)], axis=0) * scale
        s_new = jnp.where(_new_key_mask(rows, n_new, t_of_row), s_new, NEG_INF)
        pv_new = lambda p: jnp.concatenate([jnp.dot(head_rows(p, h), new_of_head(vn_ref, h), preferred_element_type=F32)
                                            for h in range(B_KV_HEADS)], axis=0)
        m, l, acc = _softmax_step(_softmax_init(rows, LANES), s_new, pv_new)
        for j in range(n_past_blocks):
            sel_j = sel[:, j:j + 1]
            mj = jnp.where(sel_j, m_sc[j][:, 0:1], NEG_INF)
            m_new = jnp.maximum(m, mj)
            alpha = jnp.exp(m - m_new)
            beta = jnp.exp(mj - m_new)
            l = alpha * l + beta * l_sc[j][:, 0:1]
            acc = alpha * acc + beta * o_sc[j]
            m = m_new
        o = acc / l
        n_grp = B_HEADS // B_KV_HEADS
        for kvh in range(B_KV_HEADS):
            for g in range(n_grp):
                h = kvh * n_grp + g
                r0 = (kvh * n_grp + g) * n_new
                o_ref[:, h * LANES:(h + 1) * LANES] = o[r0:r0 + n_new]


def _decode_call(kernel, pt_flat, pre_inputs, pre_specs, cache_k, cache_v, scratch, *, bsz, n_new, n_steps, n_pages, name):
    page_specs = _paged_specs(n_pages, n_steps)
    return pl.pallas_call(
        kernel,
        grid_spec=pltpu.PrefetchScalarGridSpec(
            num_scalar_prefetch=1,
            grid=(bsz, n_steps),
            in_specs=pre_specs + page_specs + page_specs,
            out_specs=pl.BlockSpec((None, n_new, Q_W), lambda b, s, pt: (b, 0, 0)),
            scratch_shapes=scratch),
        out_shape=jax.ShapeDtypeStruct((bsz, n_new, Q_W), F32),
        compiler_params=_params(("arbitrary", "arbitrary")),
        name=name,
    )(pt_flat, *pre_inputs, *([cache_k] * n_pages), *([cache_v] * n_pages))


def _pad_new(x3):
    return jnp.pad(x3, ((0, 0), (0, 8 - x3.shape[1]), (0, 0)))


def _diff_attn_decode(qa, ka_new, va_new, cache_k, cache_v, page_table, prm, *, n_pages_per_step):
    bsz, n_new = qa.shape[:2]
    cache_kt = jnp.transpose(cache_k[0], (0, 2, 3, 4, 1)).reshape(cache_k.shape[1], PAGE_ROWS, LANES)
    cache_v = _token_major_page(cache_v)
    n_grp = A_HEADS // A_KV_HEADS
    q6 = qa.reshape(bsz, n_new, A_KV_HEADS, n_grp, 2, A_DH).transpose(0, 2, 4, 3, 1, 5)
    eye = jnp.eye(A_KV_HEADS * 2, dtype=qa.dtype).reshape(A_KV_HEADS, 2, 1, 1, A_KV_HEADS * 2, 1)
    qbd = (q6[:, :, :, :, :, None, :] * eye[None]).reshape(bsz, A_KV_HEADS * 2 * n_grp * n_new, KV_W).astype(BF16)
    rows = qbd.shape[1]
    n_steps = page_table.shape[1] // n_pages_per_step
    lam_spec = _const_spec((1, A_DH))
    per_b = lambda r, w: pl.BlockSpec((None, r, w), lambda b, s, pt: (b, 0, 0))
    return _decode_call(
        functools.partial(_diff_decode_kernel, n_pages=n_pages_per_step, n_new=n_new),
        page_table.reshape(-1),
        [prm["a_lq1"], prm["a_lk1"], prm["a_lq2"], prm["a_lk2"], qbd, _pad_new(ka_new), _pad_new(va_new), prm["a_subln"]],
        [lam_spec, lam_spec, lam_spec, lam_spec, per_b(rows, KV_W), per_b(8, KV_W), per_b(8, KV_W), _const_spec((1, LANES))],
        cache_kt, cache_v,
        [pltpu.VMEM((rows, 1), F32), pltpu.VMEM((rows, 1), F32), pltpu.VMEM((rows, LANES), F32)],
        bsz=bsz, n_new=n_new, n_steps=n_steps, n_pages=n_pages_per_step, name="diff_attn_decode")


def _moba_decode(qb, kb_new, vb_new, cache_k, cache_v, page_table, *, n_pages_per_step):
    bsz, n_new = qb.shape[:2]
    n_grp = B_HEADS // B_KV_HEADS
    rows = B_KV_HEADS * n_grp * n_new
    qrows = qb.reshape(bsz, n_new, B_KV_HEADS, n_grp, B_DH).transpose(0, 2, 3, 1, 4).reshape(bsz, rows, B_DH)
    n_steps = page_table.shape[1] // n_pages_per_step
    n_past_blocks = page_table.shape[1] * PAGE_SIZE // MOBA_BLOCK
    assert (page_table.shape[1] * PAGE_SIZE) % MOBA_BLOCK == 0 and n_past_blocks <= LANES
    per_b = lambda r, w: pl.BlockSpec((None, r, w), lambda b, s, pt: (b, 0, 0))
    return _decode_call(
        functools.partial(_moba_decode_kernel, n_pages=n_pages_per_step, n_new=n_new, n_past_blocks=n_past_blocks),
        page_table.reshape(-1),
        [qrows, _pad_new(kb_new), _pad_new(vb_new)],
        [per_b(rows, B_DH), per_b(8, KV_W), per_b(8, KV_W)],
        _token_major_page(cache_k), _token_major_page(cache_v),
        [pltpu.VMEM((n_past_blocks, rows, LANES), F32), pltpu.VMEM((n_past_blocks, rows, LANES), F32),
         pltpu.VMEM((n_past_blocks, rows, LANES), F32), pltpu.VMEM((B_KV_HEADS, LANES, B_DH), F32)],
        bsz=bsz, n_new=n_new, n_steps=n_steps, n_pages=n_pages_per_step, name="moba_decode")


def _topk_rows(s, k, payload=None):
    n_rows = s.shape[0]
    row = lax.broadcasted_iota(I32, s.shape, 0).astype(F32)
    vals, picks = [], []
    for _ in range(k):
        mx = jnp.max(s, axis=0, keepdims=True)
        ix = jnp.min(jnp.where(s == mx, row, float(n_rows)), axis=0, keepdims=True)
        hit = row == ix
        vals.append(mx)
        picks.append(ix if payload is None else jnp.sum(jnp.where(hit, payload, 0.0), axis=0, keepdims=True))
        s = jnp.where(hit, NEG_INF, s)
    return jnp.concatenate(vals, axis=0), jnp.concatenate(picks, axis=0)


def _merge_route_kernel(x_ref, oa_ref, ob_ref, om_ref, gt_ref, wa_ref, wb_ref, wm_ref, wo_ref, fg_ref, wq_ref, keys_ref,
                        x1_ref, h2_ref, eid_ref, gw_ref):
    ya = jnp.dot(oa_ref[...].astype(BF16), wa_ref[...], preferred_element_type=F32)
    yb = jnp.dot(ob_ref[...].astype(BF16), wb_ref[...], preferred_element_type=F32)
    ym = jnp.dot(om_ref[...].astype(BF16), wm_ref[...], preferred_element_type=F32)
    merged = (gt_ref[:, 0:D_MODEL] * ya + gt_ref[:, D_MODEL:2 * D_MODEL] * yb + gt_ref[:, 2 * D_MODEL:3 * D_MODEL] * ym)
    x1 = x_ref[...] + jnp.dot(merged.astype(BF16), wo_ref[...], preferred_element_type=F32)
    x1_ref[...] = x1
    h2 = _rms(x1, fg_ref[...], D_MODEL)
    h2_ref[...] = h2
    qp = jnp.dot(h2.astype(BF16), wq_ref[...], preferred_element_type=F32)
    half = PEER_DQ // 2
    for h in range(PEER_HEADS):
        sub = []
        for p in range(2):
            qhp = qp[:, (h * 2 + p) * half:(h * 2 + p + 1) * half].astype(BF16)
            st = lax.dot_general(keys_ref[h * 2 + p], qhp, _NT, preferred_element_type=F32)
            sub.append(_topk_rows(st, PEER_TOPK))
        (s1, i1), (s2, i2) = sub
        cand_s = jnp.concatenate([s1[a:a + 1, :] + s2 for a in range(PEER_TOPK)], axis=0)
        cand_i = jnp.concatenate([i1[a:a + 1, :] * PEER_NKEYS + i2 for a in range(PEER_TOPK)], axis=0)
        top_s, expert = _topk_rows(cand_s, PEER_TOPK, payload=cand_i)
        e = jnp.exp(top_s - top_s[0:1, :])
        gw_ref[h * PEER_TOPK:(h + 1) * PEER_TOPK, :] = e / jnp.sum(e, axis=0, keepdims=True)
        eid_ref[h * PEER_TOPK:(h + 1) * PEER_TOPK, :] = expert.astype(I32) * PEER_SLAB


def _merge_route(x2d, oa, ob, om, gates, prm, *, tm):
    n = x2d.shape[0]
    row = lambda w: pl.BlockSpec((tm, w), lambda i: (i, 0))
    colblk = pl.BlockSpec((PEER_HEADS * PEER_TOPK, tm), lambda i: (0, i))
    sq = _const_spec((D_MODEL, D_MODEL))
    return pl.pallas_call(
        _merge_route_kernel,
        grid=(n // tm,),
        in_specs=[row(D_MODEL), row(Q_W), row(Q_W), row(Q_W), row(GATE_W), sq, sq, sq, sq, _const_spec((1, D_MODEL)),
                  _const_spec((D_MODEL, PEER_HEADS * PEER_DQ)), _const_spec((PEER_HEADS * 2, PEER_NKEYS, PEER_DQ // 2))],
        out_specs=[row(D_MODEL), row(D_MODEL), colblk, colblk],
        out_shape=[jax.ShapeDtypeStruct((n, D_MODEL), F32), jax.ShapeDtypeStruct((n, D_MODEL), F32),
                   jax.ShapeDtypeStruct((PEER_HEADS * PEER_TOPK, n), I32), jax.ShapeDtypeStruct((PEER_HEADS * PEER_TOPK, n), F32)],
        compiler_params=_params(("arbitrary",)),
        name="merge_route",
    )(x2d, oa, ob, om, gates, prm["w_br_a"], prm["w_br_b"], prm["w_br_m"], prm["w_out"], prm["ffn_norm_g"],
      prm["w_peer_q"], prm["peer_keys"])


def _pack_table(tab):
    bits = lax.bitcast_convert_type(tab.astype(jnp.bfloat16), jnp.uint16).astype(jnp.uint32)
    half = tab.shape[1] // 2
    words = bits[:, :half] | (bits[:, half:] << 16)
    return lax.bitcast_convert_type(words, I32).reshape(tab.shape[0] * PEER_SLAB, LANES)


def _unpack_row(tab_ref, row0):
    slab = tab_ref[pl.ds(pl.multiple_of(row0, PEER_SLAB), PEER_SLAB), :]
    lo = lax.bitcast_convert_type(slab << 16, F32)
    hi = lax.bitcast_convert_type(slab & jnp.int32(-65536), F32)
    return lo, hi


def _peer_act_kernel(eid_ref, x_ref, gw_ref, fold_ref, tab_ref, w_ref, s_sc, z_sc, *, tt):
    n_k = PEER_HEADS * PEER_TOPK

    def products(t, carry):
        x8 = x_ref[t]
        xlo, xhi = x8[0:PEER_SLAB], x8[PEER_SLAB:2 * PEER_SLAB]
        experts = eid_ref.at[t]
        slots = s_sc.at[t]
        for k in range(n_k):
            lo, hi = _unpack_row(tab_ref, experts[k])
            slots[k * PEER_SLAB:(k + 1) * PEER_SLAB, :] = lo * xlo + hi * xhi
        return carry

    def lane_sums(i, carry):
        for u in range(PEER_XPOSE_UNROLL):
            t = i * PEER_XPOSE_UNROLL + u
            z_sc[pl.ds(t, 1), :] = jnp.sum(s_sc[t].T, axis=0, keepdims=True)
        return carry

    lax.fori_loop(0, tt, products, 0)
    lax.fori_loop(0, tt // PEER_XPOSE_UNROLL, lane_sums, 0)
    z = z_sc[...]
    z_hi = z.astype(BF16)
    z_lo = (z - z_hi.astype(F32)).astype(BF16)
    a = (jnp.dot(z_hi, fold_ref[...], preferred_element_type=F32) + jnp.dot(z_lo, fold_ref[...], preferred_element_type=F32))
    gelu = 0.5 * a * (1.0 + lax.erf(a * (2.0 ** -0.5)))
    w_ref[...] = gw_ref[...] * gelu


def _peer_out_kernel(eid_ref, w_ref, x1_ref, tab_ref, y_ref, wb_sc, *, tt):
    n_k = PEER_HEADS * PEER_TOPK

    def spread(i, carry):
        for u in range(PEER_XPOSE_UNROLL):
            t = i * PEER_XPOSE_UNROLL + u
            wb_sc[t] = jnp.broadcast_to(w_ref[pl.ds(t, 1), :], (n_k, n_k)).T
        return carry

    lax.fori_loop(0, tt // PEER_XPOSE_UNROLL, spread, 0)

    def token(t, carry):
        experts = eid_ref.at[t]
        weights = wb_sc.at[t]
        acc = [[jnp.zeros((PEER_SLAB, LANES), F32), jnp.zeros((PEER_SLAB, LANES), F32)] for _ in range(2)]
        for k in range(n_k):
            lo, hi = _unpack_row(tab_ref, experts[k])
            w = weights[k:k + 1, :]
            acc[k % 2][0] = acc[k % 2][0] + w * lo
            acc[k % 2][1] = acc[k % 2][1] + w * hi
        y_ref[t] = x1_ref[t] + jnp.concatenate([acc[0][0] + acc[1][0], acc[0][1] + acc[1][1]], axis=0)
        return carry

    lax.fori_loop(0, tt, token, 0)


def _peer(h2, x1, eid, gw, u_pack, v_pack, *, tt):
    n = h2.shape[0]
    n_k = PEER_HEADS * PEER_TOPK
    rows8 = D_MODEL // LANES
    smem = lambda: pl.BlockSpec((tt, n_k), lambda i: (i, 0), memory_space=pltpu.SMEM)
    tok3 = pl.BlockSpec((tt, rows8, LANES), lambda i: (i, 0, 0))
    tab_spec = _const_spec((PEER_N * PEER_SLAB, LANES), True)
    fold = (jnp.arange(n_k * PEER_SLAB)[:, None] // PEER_SLAB == jnp.arange(n_k)[None, :]).astype(BF16)
    w = pl.pallas_call(
        functools.partial(_peer_act_kernel, tt=tt),
        grid=(n // tt,),
        in_specs=[smem(), tok3, pl.BlockSpec((tt, n_k), lambda i: (i, 0)), _const_spec((n_k * PEER_SLAB, n_k)), tab_spec],
        out_specs=pl.BlockSpec((tt, n_k), lambda i: (i, 0)),
        out_shape=jax.ShapeDtypeStruct((n, n_k), F32),
        scratch_shapes=[pltpu.VMEM((tt, n_k * PEER_SLAB, LANES), F32), pltpu.VMEM((tt, n_k * PEER_SLAB), F32)],
        compiler_params=_params(("arbitrary",)),
        name="peer_act",
    )(eid, h2.reshape(n, rows8, LANES), gw, fold, u_pack)
    y = pl.pallas_call(
        functools.partial(_peer_out_kernel, tt=tt),
        grid=(n // tt,),
        in_specs=[smem(), pl.BlockSpec((tt, n_k), lambda i: (i, 0)), tok3, tab_spec],
        out_specs=tok3,
        out_shape=jax.ShapeDtypeStruct((n, rows8, LANES), F32),
        scratch_shapes=[pltpu.VMEM((tt, n_k, n_k), F32)],
        compiler_params=_params(("arbitrary",)),
        name="peer_out",
    )(eid, w, x1.reshape(n, rows8, LANES), v_pack)
    return y.reshape(n, D_MODEL)


def _merge_ffn(x2d, oa, ob, om, gates, prm, u_pack, v_pack, *, tm, tt):
    x1, h2, eid_t, gw_t = _merge_route(x2d, oa, ob, om, gates, prm, tm=tm)
    return _peer(h2, x1, eid_t.T, gw_t.T, u_pack, v_pack, tt=tt)


def kernel(x_prompt, x_sample, mem_prompt, cache_a_k, cache_a_v, cache_b_k, cache_b_v, cache_mem_k, cache_mem_v, page_table, attn_norm_g, w_in, a_q_norm, a_k_norm, a_lq1, a_lk1, a_lq2, a_lk2, a_subln, b_q_norm, b_k_norm, mem_norm_g, w_mem_kv, m_q_norm, m_k_norm, w_br_a, w_br_b, w_br_m, w_out, ffn_norm_g, w_peer_q, peer_keys, peer_u, peer_v):
    assert attn_norm_g.shape[0] == 1, "single trunk layer"
    bsz, seq, _ = x_prompt.shape
    dbsz, n_new, _ = x_sample.shape
    n_mem = mem_prompt.shape[1]
    n_pool = cache_a_k.shape[1]
    past = page_table.shape[1] * PAGE_SIZE

    prm = dict(
        attn_norm_g=attn_norm_g, w_in=w_in[0].astype(BF16),
        a_q_norm=jnp.tile(a_q_norm, (1, 2)), a_k_norm=jnp.tile(a_k_norm, (1, 2)),
        a_lq1=a_lq1, a_lk1=a_lk1, a_lq2=a_lq2, a_lk2=a_lk2, a_subln=a_subln,
        b_q_norm=b_q_norm, b_k_norm=b_k_norm, mem_norm_g=mem_norm_g, w_mem_kv=w_mem_kv[0].astype(BF16),
        m_q_norm=m_q_norm, m_k_norm=m_k_norm,
        w_br_a=w_br_a[0].astype(BF16), w_br_b=w_br_b[0].astype(BF16), w_br_m=w_br_m[0].astype(BF16),
        w_out=w_out[0].astype(BF16), ffn_norm_g=ffn_norm_g, w_peer_q=w_peer_q[0].astype(BF16),
        peer_keys=peer_keys[0].reshape(PEER_HEADS * 2, PEER_NKEYS, PEER_DQ // 2).astype(BF16))
    u_pack = _pack_table(peer_u[0])
    v_pack = _pack_table(peer_v[0])

    tm = 256
    pos_p = jnp.arange(seq, dtype=I32)
    xp2 = x_prompt.reshape(bsz * seq, D_MODEL)
    qa, ka, va, qb, kb, vb, qm, gates = _in_proj(xp2, _rope_table(pos_p, A_DH), _rope_table(pos_p, B_DH), prm, tm=tm, q_dtype=BF16)
    r3 = lambda a: a.reshape(bsz, seq, a.shape[-1])
    oa = _diff_attn_prompt(r3(qa), r3(ka), r3(va), prm, tq=256)
    ob = _moba_prompt(r3(qb), r3(kb), r3(vb))
    mk, mv = _mem_kv(mem_prompt.reshape(bsz * n_mem, D_MODEL), prm, tm=tm)
    om = _mem_attn(r3(qm), mk.reshape(bsz, n_mem, Q_W), mv.reshape(bsz, n_mem, Q_W), tq=256, out_dtype=BF16)
    y_prompt = _merge_ffn(xp2, oa.reshape(-1, Q_W), ob.reshape(-1, Q_W), om.reshape(-1, Q_W), gates, prm, u_pack, v_pack, tm=tm, tt=32)

    pos_s = jnp.tile(past + jnp.arange(n_new, dtype=I32), tm // n_new)
    xs2 = x_sample.reshape(dbsz * n_new, D_MODEL)
    sqa, ska, sva, sqb, skb, svb, sqm, sgates = _in_proj(xs2, _rope_table(pos_s, A_DH), _rope_table(pos_s, B_DH), prm, tm=tm, q_dtype=F32)
    s3 = lambda a: a.reshape(dbsz, n_new, a.shape[-1])
    soa = _diff_attn_decode(s3(sqa), s3(ska), s3(sva), cache_a_k, cache_a_v, page_table, prm, n_pages_per_step=8)
    sob = _moba_decode(s3(sqb), s3(skb), s3(svb), cache_b_k, cache_b_v, page_table, n_pages_per_step=8)
    som = _mem_attn(s3(sqm), cache_mem_k[0].reshape(dbsz, n_mem, Q_W), cache_mem_v[0].reshape(dbsz, n_mem, Q_W), tq=n_new, out_dtype=F32)
    y_sample = _merge_ffn(xs2, soa.reshape(-1, Q_W), sob.reshape(-1, Q_W), som.reshape(-1, Q_W), sgates, prm, u_pack, v_pack, tm=tm, tt=32)

    return (y_prompt.reshape(bsz, seq, D_MODEL), y_sample.reshape(dbsz, n_new, D_MODEL),
            ka.reshape(1, bsz, seq, A_KV_HEADS, 2, A_DH), va.reshape(1, bsz, seq, A_KV_HEADS, 2 * A_DH),
            kb.reshape(1, bsz, seq, B_KV_HEADS, B_DH), vb.reshape(1, bsz, seq, B_KV_HEADS, B_DH),
            mk.reshape(1, bsz, n_mem, M_HEADS, M_DH), mv.reshape(1, bsz, n_mem, M_HEADS, M_DH),
            ska.reshape(1, dbsz, n_new, A_KV_HEADS, 2, A_DH), sva.reshape(1, dbsz, n_new, A_KV_HEADS, 2 * A_DH),
            skb.reshape(1, dbsz, n_new, B_KV_HEADS, B_DH), svb.reshape(1, dbsz, n_new, B_KV_HEADS, B_DH))
```

```python
import functools
import math

import jax
import jax.numpy as jnp
from jax import lax
from jax.experimental import pallas as pl
from jax.experimental.pallas import tpu as pltpu

F32 = jnp.float32
BF16 = jnp.bfloat16
I32 = jnp.int32
NEG_INF = float("-inf")

D_MODEL = 1024
PAGE_SIZE = 128
A_HEADS, A_KV_HEADS, A_DH = 8, 4, 64
B_HEADS, B_KV_HEADS, B_DH = 8, 4, 128
MOBA_BLOCK, MOBA_TOPK = 256, 3
M_HEADS, M_DH = 4, 256
ROPE_THETA, ROT_DIV = 500000.0, 4
PEER_HEADS, PEER_NKEYS, PEER_DQ, PEER_TOPK = 8, 128, 256, 16
PEER_N = PEER_NKEYS * PEER_NKEYS
EPS = 1e-6
LAM_INIT = 0.8 - 0.6 * math.exp(-0.3 * 0)

LANES = 128
KV_W = 512
PAGE_ROWS = PAGE_SIZE * KV_W // LANES
Q_W = 1024
GATE_W = 3 * D_MODEL
IN_W = 3 * Q_W + 4 * KV_W + GATE_W
PEER_SLAB = 4
PEER_XPOSE_UNROLL = 4
VMEM_LIMIT = 56 * 1024 * 1024

_NT = (((1,), (1,)), ((), ()))


def _const_spec(shape, single_buffer=False):
    index_map = lambda *_: (0,) * len(shape)
    if single_buffer:
        return pl.BlockSpec(shape, index_map, pipeline_mode=pl.Buffered(1))
    return pl.BlockSpec(shape, index_map)


def _params(sem):
    return pltpu.CompilerParams(dimension_semantics=sem, vmem_limit_bytes=VMEM_LIMIT)


def _rms(x, g, width):
    return x * lax.rsqrt(jnp.sum(x * x, axis=-1, keepdims=True) * (1.0 / width) + EPS) * g


def _rope_table(pos, seg):
    r = seg // ROT_DIV
    half = r // 2
    inv = ROPE_THETA ** (-jnp.arange(half, dtype=F32) * (2.0 / r))
    ang = pos.astype(F32)[:, None] * inv[None, :]
    cos, sin = jnp.cos(ang), jnp.sin(ang)
    n = pos.shape[0]
    z_half = jnp.zeros((n, half), F32)
    z_rest = jnp.zeros((n, seg - r), F32)
    c = jnp.concatenate([cos, cos, jnp.ones((n, seg - r), F32)], axis=1)
    sm = jnp.concatenate([-sin, z_half, z_rest], axis=1)
    sp = jnp.concatenate([z_half, sin, z_rest], axis=1)
    reps = LANES // seg
    return jnp.concatenate([jnp.tile(c, (1, reps)), jnp.tile(sm, (1, reps)), jnp.tile(sp, (1, reps))], axis=1)


def _rope(y, tab_ref, half):
    c = tab_ref[:, 0:LANES]
    sm = tab_ref[:, LANES:2 * LANES]
    sp = tab_ref[:, 2 * LANES:3 * LANES]
    return y * c + pltpu.roll(y, LANES - half, 1) * sm + pltpu.roll(y, half, 1) * sp


def _in_proj_kernel(x_ref, g_ref, w_ref, ropea_ref, ropeb_ref, aqn_ref, akn_ref, bqn_ref, bkn_ref, mqn_ref,
                    qa_ref, ka_ref, va_ref, qb_ref, kb_ref, vb_ref, qm_ref, gt_ref):
    x = x_ref[...]
    hb = _rms(x, g_ref[...], D_MODEL).astype(BF16)
    tm = x.shape[0]
    lane = lax.broadcasted_iota(I32, (tm, LANES), 1)
    lo = lane < A_DH

    def mm(c0, width):
        return jnp.dot(hb, w_ref[:, c0:c0 + width], preferred_element_type=F32)

    def norm64(y, gn):
        ss = y * y
        s_lo = jnp.sum(jnp.where(lo, ss, 0.0), axis=-1, keepdims=True)
        s_hi = jnp.sum(jnp.where(lo, 0.0, ss), axis=-1, keepdims=True)
        inv = jnp.where(lo, lax.rsqrt(s_lo * (1.0 / A_DH) + EPS), lax.rsqrt(s_hi * (1.0 / A_DH) + EPS))
        return y * inv * gn

    c0 = 0
    y = mm(c0, Q_W)
    for b in range(Q_W // LANES):
        blk = slice(b * LANES, (b + 1) * LANES)
        qa_ref[:, blk] = (_rope(norm64(y[:, blk], aqn_ref[...]), ropea_ref, A_DH // ROT_DIV // 2)
                          * (A_DH ** -0.5)).astype(qa_ref.dtype)
    c0 += Q_W
    y = mm(c0, KV_W)
    for b in range(KV_W // LANES):
        blk = slice(b * LANES, (b + 1) * LANES)
        ka_ref[:, blk] = _rope(norm64(y[:, blk], akn_ref[...]), ropea_ref, A_DH // ROT_DIV // 2)
    c0 += KV_W
    va_ref[...] = mm(c0, KV_W)
    c0 += KV_W
    y = mm(c0, Q_W)
    for b in range(Q_W // LANES):
        blk = slice(b * LANES, (b + 1) * LANES)
        qb_ref[:, blk] = _rope(_rms(y[:, blk], bqn_ref[...], B_DH), ropeb_ref, B_DH // ROT_DIV // 2)
    c0 += Q_W
    y = mm(c0, KV_W)
    for b in range(KV_W // LANES):
        blk = slice(b * LANES, (b + 1) * LANES)
        kb_ref[:, blk] = _rope(_rms(y[:, blk], bkn_ref[...], B_DH), ropeb_ref, B_DH // ROT_DIV // 2)
    c0 += KV_W
    vb_ref[...] = mm(c0, KV_W)
    c0 += KV_W
    y = mm(c0, Q_W)
    for h in range(M_HEADS):
        blk = slice(h * M_DH, (h + 1) * M_DH)
        qm_ref[:, blk] = (_rms(y[:, blk], mqn_ref[...], M_DH) * (M_DH ** -0.5)).astype(qm_ref.dtype)
    c0 += Q_W
    for b in range(3):
        gt_ref[:, b * D_MODEL:(b + 1) * D_MODEL] = jax.nn.sigmoid(mm(c0 + b * D_MODEL, D_MODEL))


def _in_proj(x2d, rope_a, rope_b, prm, *, tm, q_dtype):
    n = x2d.shape[0]
    n_rope = rope_a.shape[0] // tm
    row = lambda w: pl.BlockSpec((tm, w), lambda i: (i, 0))
    rope_spec = pl.BlockSpec((tm, 3 * LANES), lambda i: (i % n_rope, 0))
    out_shape = [jax.ShapeDtypeStruct((n, Q_W), q_dtype), jax.ShapeDtypeStruct((n, KV_W), F32),
                 jax.ShapeDtypeStruct((n, KV_W), F32), jax.ShapeDtypeStruct((n, Q_W), F32),
                 jax.ShapeDtypeStruct((n, KV_W), F32), jax.ShapeDtypeStruct((n, KV_W), F32),
                 jax.ShapeDtypeStruct((n, Q_W), q_dtype), jax.ShapeDtypeStruct((n, GATE_W), F32)]
    return pl.pallas_call(
        _in_proj_kernel,
        grid=(n // tm,),
        in_specs=[row(D_MODEL), _const_spec((1, D_MODEL)), _const_spec((D_MODEL, IN_W), True), rope_spec, rope_spec,
                  _const_spec((1, LANES)), _const_spec((1, LANES)), _const_spec((1, LANES)), _const_spec((1, LANES)),
                  _const_spec((1, M_DH))],
        out_specs=[row(Q_W), row(KV_W), row(KV_W), row(Q_W), row(KV_W), row(KV_W), row(Q_W), row(GATE_W)],
        out_shape=out_shape,
        compiler_params=_params(("arbitrary",)),
        name="in_proj",
    )(x2d, prm["attn_norm_g"], prm["w_in"], rope_a, rope_b, prm["a_q_norm"], prm["a_k_norm"],
      prm["b_q_norm"], prm["b_k_norm"], prm["m_q_norm"])


def _diff_lambda(lq1_ref, lk1_ref, lq2_ref, lk2_ref):
    return (jnp.exp(jnp.sum(lq1_ref[...] * lk1_ref[...], keepdims=True))
            - jnp.exp(jnp.sum(lq2_ref[...] * lk2_ref[...], keepdims=True)) + LAM_INIT)


def _softmax_step(carry, s, v):
    m, l, acc = carry
    m_new = jnp.maximum(m, jnp.max(s, axis=-1, keepdims=True))
    alpha = jnp.exp(m - m_new)
    p = jnp.exp(s - m_new)
    l = alpha * l + jnp.sum(p, axis=-1, keepdims=True)
    pv = v(p.astype(BF16)) if callable(v) else jnp.dot(p.astype(BF16), v, preferred_element_type=F32)
    return m_new, l, alpha * acc + pv


def _softmax_init(rows, width):
    return (jnp.full((rows, 1), NEG_INF, F32), jnp.zeros((rows, 1), F32), jnp.zeros((rows, width), F32))


def _diff_attn_kernel(lq1_ref, lk1_ref, lq2_ref, lk2_ref, q_ref, k_ref, v_ref, subln_ref, o_ref, kb_sc, vb_sc, *, tq):
    qi = pl.program_id(2)

    @pl.when(qi == 0)
    def _cast_kv():
        kb_sc[...] = k_ref[...].astype(BF16)
        vb_sc[...] = v_ref[...].astype(BF16)

    lam = _diff_lambda(lq1_ref, lk1_ref, lq2_ref, lk2_ref)
    lane = lax.broadcasted_iota(I32, (tq, LANES), 1)
    r = lax.broadcasted_iota(I32, (2 * tq, tq), 0)
    r = jnp.where(r >= tq, r - tq, r)
    causal = r >= lax.broadcasted_iota(I32, (2 * tq, tq), 1)

    for g in range(A_HEADS // A_KV_HEADS):
        qg = q_ref[:, g * LANES:(g + 1) * LANES]
        zero = jnp.zeros_like(qg)
        qq = jnp.concatenate([jnp.where(lane < A_DH, qg, zero), jnp.where(lane >= A_DH, qg, zero)], axis=0)

        def step(j, carry, masked):
            kj = kb_sc[pl.ds(pl.multiple_of(j * tq, tq), tq), :]
            vj = vb_sc[pl.ds(pl.multiple_of(j * tq, tq), tq), :]
            s = lax.dot_general(qq, kj, _NT, preferred_element_type=F32)
            if masked:
                s = jnp.where(causal, s, NEG_INF)
            return _softmax_step(carry, s, vj)

        carry = step(qi, _softmax_init(2 * tq, LANES), True)
        m, l, acc = lax.fori_loop(0, qi, lambda j, c: step(j, c, False), carry)
        o = acc / l
        o = o[:tq] - lam * o[tq:]
        o = _rms(o, subln_ref[...], 2 * A_DH) * (1.0 - LAM_INIT)
        o_ref[:, g * LANES:(g + 1) * LANES] = o.astype(o_ref.dtype)


def _diff_attn_prompt(qa, ka, va, prm, *, tq):
    bsz, seq = qa.shape[:2]
    grp = (A_HEADS // A_KV_HEADS) * LANES
    lam_spec = _const_spec((1, A_DH))
    return pl.pallas_call(
        functools.partial(_diff_attn_kernel, tq=tq),
        grid=(bsz, A_KV_HEADS, seq // tq),
        in_specs=[lam_spec, lam_spec, lam_spec, lam_spec,
                  pl.BlockSpec((None, tq, grp), lambda b, h, i: (b, i, h)),
                  pl.BlockSpec((None, seq, LANES), lambda b, h, i: (b, 0, h)),
                  pl.BlockSpec((None, seq, LANES), lambda b, h, i: (b, 0, h)),
                  _const_spec((1, LANES))],
        out_specs=pl.BlockSpec((None, tq, grp), lambda b, h, i: (b, i, h)),
        out_shape=jax.ShapeDtypeStruct((bsz, seq, Q_W), BF16),
        scratch_shapes=[pltpu.VMEM((seq, LANES), BF16), pltpu.VMEM((seq, LANES), BF16)],
        compiler_params=_params(("arbitrary", "arbitrary", "arbitrary")),
        name="diff_attn_prompt",
    )(prm["a_lq1"], prm["a_lk1"], prm["a_lq2"], prm["a_lk2"], qa, ka, va, prm["a_subln"])


def _select_topk_lanes(gate, k):
    col = lax.broadcasted_iota(I32, gate.shape, 1).astype(F32)
    sel = jnp.zeros(gate.shape, jnp.bool_)
    for _ in range(k):
        mx = jnp.max(gate, axis=-1, keepdims=True)
        ix = jnp.min(jnp.where(gate == mx, col, float(LANES)), axis=-1, keepdims=True)
        hit = col == ix
        sel = sel | (hit & (mx > NEG_INF))
        gate = jnp.where(hit, NEG_INF, gate)
    return sel


def _moba_kernel(q_ref, k_ref, v_ref, o_ref, kb_sc, vb_sc, km_sc, *, tq, n_blocks):
    qi = pl.program_id(2)

    @pl.when(qi == 0)
    def _prep_kv():
        kb_sc[...] = k_ref[...].astype(BF16)
        vb_sc[...] = v_ref[...].astype(BF16)
        km_sc[...] = jnp.zeros_like(km_sc)
        for j in range(n_blocks):
            km_sc[j:j + 1, :] = jnp.mean(k_ref[j * MOBA_BLOCK:(j + 1) * MOBA_BLOCK, :], axis=0, keepdims=True)

    n_grp = B_HEADS // B_KV_HEADS
    qf = jnp.concatenate([q_ref[:, g * LANES:(g + 1) * LANES] for g in range(n_grp)], axis=0)
    rows = n_grp * tq
    gate = lax.dot_general(qf, km_sc[...], _NT, precision=lax.Precision.HIGHEST, preferred_element_type=F32)
    col = lax.broadcasted_iota(I32, (rows, LANES), 1)
    gate = jnp.where(col < qi, gate, NEG_INF)
    sel = _select_topk_lanes(gate, MOBA_TOPK).astype(F32)

    qb = qf.astype(BF16)
    r = lax.broadcasted_iota(I32, (rows, tq), 0)
    for g in range(1, n_grp):
        r = jnp.where(r >= g * tq, r - tq, r)
    causal = r >= lax.broadcasted_iota(I32, (rows, tq), 1)
    scale = B_DH ** -0.5

    def step(j, carry, own):
        kj = kb_sc[pl.ds(pl.multiple_of(j * tq, tq), tq), :]
        vj = vb_sc[pl.ds(pl.multiple_of(j * tq, tq), tq), :]
        s = lax.dot_general(qb, kj, _NT, preferred_element_type=F32) * scale
        if own:
            s = jnp.where(causal, s, NEG_INF)
        else:
            sel_j = jnp.sum(jnp.where(col == j, sel, 0.0), axis=-1, keepdims=True) > 0.0
            s = jnp.where(sel_j, s, NEG_INF)
        return _softmax_step(carry, s, vj)

    carry = step(qi, _softmax_init(rows, LANES), True)
    m, l, acc = lax.fori_loop(0, qi, lambda j, c: step(j, c, False), carry)
    o = acc / l
    for g in range(n_grp):
        o_ref[:, g * LANES:(g + 1) * LANES] = o[g * tq:(g + 1) * tq].astype(o_ref.dtype)


def _moba_prompt(qb, kb, vb):
    bsz, seq = qb.shape[:2]
    tq = MOBA_BLOCK
    n_blocks = seq // MOBA_BLOCK
    assert seq % MOBA_BLOCK == 0 and n_blocks <= LANES
    grp = (B_HEADS // B_KV_HEADS) * LANES
    return pl.pallas_call(
        functools.partial(_moba_kernel, tq=tq, n_blocks=n_blocks),
        grid=(bsz, B_KV_HEADS, seq // tq),
        in_specs=[pl.BlockSpec((None, tq, grp), lambda b, h, i: (b, i, h)),
                  pl.BlockSpec((None, seq, LANES), lambda b, h, i: (b, 0, h)),
                  pl.BlockSpec((None, seq, LANES), lambda b, h, i: (b, 0, h))],
        out_specs=pl.BlockSpec((None, tq, grp), lambda b, h, i: (b, i, h)),
        out_shape=jax.ShapeDtypeStruct((bsz, seq, Q_W), BF16),
        scratch_shapes=[pltpu.VMEM((seq, LANES), BF16), pltpu.VMEM((seq, LANES), BF16), pltpu.VMEM((LANES, LANES), F32)],
        compiler_params=_params(("arbitrary", "arbitrary", "arbitrary")),
        name="moba_prompt",
    )(qb, kb, vb)


def _mem_kv_kernel(x_ref, g_ref, w_ref, kn_ref, k_ref, v_ref):
    hb = _rms(x_ref[...], g_ref[...], D_MODEL).astype(BF16)
    y = jnp.dot(hb, w_ref[:, 0:Q_W], preferred_element_type=F32)
    for h in range(M_HEADS):
        blk = slice(h * M_DH, (h + 1) * M_DH)
        k_ref[:, blk] = _rms(y[:, blk], kn_ref[...], M_DH)
    v_ref[...] = jnp.dot(hb, w_ref[:, Q_W:2 * Q_W], preferred_element_type=F32)


def _mem_kv(mem2d, prm, *, tm):
    n = mem2d.shape[0]
    row = pl.BlockSpec((tm, D_MODEL), lambda i: (i, 0))
    return pl.pallas_call(
        _mem_kv_kernel,
        grid=(n // tm,),
        in_specs=[row, _const_spec((1, D_MODEL)), _const_spec((D_MODEL, 2 * Q_W)), _const_spec((1, M_DH))],
        out_specs=[row, row],
        out_shape=[jax.ShapeDtypeStruct((n, Q_W), F32), jax.ShapeDtypeStruct((n, Q_W), F32)],
        compiler_params=_params(("arbitrary",)),
        name="mem_kv",
    )(mem2d, prm["mem_norm_g"], prm["w_mem_kv"], prm["m_k_norm"])


def _mem_attn_kernel(q_ref, k_ref, v_ref, o_ref, *, head_rows):
    n_halves = M_DH // LANES
    per_token = M_HEADS * n_halves
    n_mem = k_ref.shape[0] // per_token if head_rows else k_ref.shape[0]
    for h in range(M_HEADS):
        blk = slice(h * M_DH, (h + 1) * M_DH)
        q = q_ref[:, blk].astype(BF16)
        if head_rows:
            pick = lambda ref: jnp.concatenate([ref[pl.ds(hf * M_HEADS + h, n_mem, stride=per_token), :]
                                                for hf in range(n_halves)], axis=1)
            kh, vh = pick(k_ref), pick(v_ref)
        else:
            kh, vh = k_ref[:, blk], v_ref[:, blk]
        s = lax.dot_general(q, kh.astype(BF16), _NT, preferred_element_type=F32)
        p = jnp.exp(s - jnp.max(s, axis=-1, keepdims=True))
        p = p / jnp.sum(p, axis=-1, keepdims=True)
        o_ref[:, blk] = jnp.dot(p.astype(BF16), vh.astype(BF16), preferred_element_type=F32).astype(o_ref.dtype)


def _mem_attn(qm, mk, mv, *, tq, out_dtype):
    bsz, seq = qm.shape[:2]
    head_rows = mk.shape[2] == LANES
    return pl.pallas_call(
        functools.partial(_mem_attn_kernel, head_rows=head_rows),
        grid=(bsz, seq // tq),
        in_specs=[pl.BlockSpec((None, tq, Q_W), lambda b, i: (b, i, 0)),
                  pl.BlockSpec((None,) + mk.shape[1:], lambda b, i: (b, 0, 0)),
                  pl.BlockSpec((None,) + mv.shape[1:], lambda b, i: (b, 0, 0))],
        out_specs=pl.BlockSpec((None, tq, Q_W), lambda b, i: (b, i, 0)),
        out_shape=jax.ShapeDtypeStruct((bsz, seq, Q_W), out_dtype),
        compiler_params=_params(("arbitrary", "arbitrary")),
        name="mem_attn",
    )(qm, mk, mv)


def _paged_specs(n_page_ops, n_steps):
    def spec(p):
        return pl.BlockSpec((None, PAGE_ROWS, LANES), lambda b, s, pt: (pt[b * (n_steps * n_page_ops) + s * n_page_ops + p], 0, 0))
    return [spec(p) for p in range(n_page_ops)]


def _token_major_page(cache):
    return cache[0].reshape(cache.shape[1], PAGE_ROWS, LANES)


def _head_rows(ref, h, n_heads):
    return ref[pl.ds(h, PAGE_SIZE, stride=n_heads), :]


def _new_key_mask(rows, n_new, t_of_row):
    col = lax.broadcasted_iota(I32, (rows, 8), 1)
    return (col <= t_of_row) & (col < n_new)


def _diff_decode_kernel(pt_ref, lq1_ref, lk1_ref, lq2_ref, lk2_ref, q_ref, kn_ref, vn_ref, subln_ref, *rest, n_pages, n_new):
    kt_refs, v_refs = rest[:n_pages], rest[n_pages:2 * n_pages]
    o_ref, m_sc, l_sc, acc_sc = rest[2 * n_pages:]
    s_id = pl.program_id(1)
    q = q_ref[...]
    rows = q.shape[0]
    rows_per_kvh = rows // A_KV_HEADS

    @pl.when(s_id == 0)
    def _init():
        m_sc[...] = jnp.full(m_sc.shape, NEG_INF, F32)
        l_sc[...] = jnp.zeros(l_sc.shape, F32)
        acc_sc[...] = jnp.zeros(acc_sc.shape, F32)

    def by_kv_head(p, value_of_head):
        return jnp.concatenate([jnp.dot(p[h * rows_per_kvh:(h + 1) * rows_per_kvh], value_of_head(h), preferred_element_type=F32)
                                for h in range(A_KV_HEADS)], axis=0)

    def paged_values(h):
        return jnp.concatenate([_head_rows(r, h, A_KV_HEADS) for r in v_refs], axis=0).astype(BF16)

    s = jnp.concatenate([jnp.dot(q, r[...].astype(BF16), preferred_element_type=F32) for r in kt_refs], axis=1)
    m, l, acc = _softmax_step((m_sc[...], l_sc[...], acc_sc[...]), s, lambda p: by_kv_head(p, paged_values))
    m_sc[...] = m
    l_sc[...] = l
    acc_sc[...] = acc

    @pl.when(s_id == pl.num_programs(1) - 1)
    def _finish():
        lam = _diff_lambda(lq1_ref, lk1_ref, lq2_ref, lk2_ref)
        t_of_row = lax.broadcasted_iota(I32, (rows, 8), 0) % n_new
        s_new = lax.dot_general(q, kn_ref[...].astype(BF16), _NT, preferred_element_type=F32)
        s_new = jnp.where(_new_key_mask(rows, n_new, t_of_row), s_new, NEG_INF)
        new_values = lambda h: vn_ref[:, h * LANES:(h + 1) * LANES].astype(BF16)
        m2, l2, acc2 = _softmax_step((m_sc[...], l_sc[...], acc_sc[...]), s_new, lambda p: by_kv_head(p, new_values))
        o = acc2 / l2
        n_grp = A_HEADS // A_KV_HEADS
        per_map = n_grp * n_new
        for kvh in range(A_KV_HEADS):
            base = kvh * 2 * per_map
            od = o[base:base + per_map] - lam * o[base + per_map:base + 2 * per_map]
            od = _rms(od, subln_ref[...], 2 * A_DH) * (1.0 - LAM_INIT)
            for g in range(n_grp):
                h = kvh * n_grp + g
                o_ref[:, h * LANES:(h + 1) * LANES] = od[g * n_new:(g + 1) * n_new]


def _moba_decode_kernel(pt_ref, q_ref, kn_ref, vn_ref, *rest, n_pages, n_new, n_past_blocks):
    k_refs, v_refs = rest[:n_pages], rest[n_pages:2 * n_pages]
    o_ref, m_sc, l_sc, o_sc, km_sc = rest[2 * n_pages:]
    s_id = pl.program_id(1)
    qf = q_ref[...]
    qb = qf.astype(BF16)
    rows = qf.shape[0]
    rows_per_kvh = rows // B_KV_HEADS
    scale = B_DH ** -0.5
    pages_per_block = MOBA_BLOCK // PAGE_SIZE
    blocks_per_step = n_pages // pages_per_block
    block_rows = MOBA_BLOCK * B_KV_HEADS
    same_head = (lax.broadcasted_iota(I32, (rows, block_rows), 1) % B_KV_HEADS
                 == lax.broadcasted_iota(I32, (rows, block_rows), 0) // rows_per_kvh)
    head_rows = lambda a, h: a[h * rows_per_kvh:(h + 1) * rows_per_kvh]

    @pl.when(s_id == 0)
    def _init():
        km_sc[...] = jnp.zeros_like(km_sc)

    for jj in range(blocks_per_step):
        j = s_id * blocks_per_step + jj
        kf = jnp.concatenate([k_refs[jj * pages_per_block + p][...] for p in range(pages_per_block)], axis=0)
        vb = jnp.concatenate([v_refs[jj * pages_per_block + p][...] for p in range(pages_per_block)], axis=0).astype(BF16)
        ksum = jnp.sum(kf.reshape(block_rows // 8, 8, LANES), axis=0)
        kmean = (ksum[0:B_KV_HEADS] + ksum[B_KV_HEADS:2 * B_KV_HEADS]) * (1.0 / MOBA_BLOCK)
        for h in range(B_KV_HEADS):
            km_sc[h, pl.ds(j, 1), :] = kmean[h:h + 1]
        s = lax.dot_general(qb, kf.astype(BF16), _NT, preferred_element_type=F32) * scale
        s = jnp.where(same_head, s, NEG_INF)
        m = jnp.max(s, axis=-1, keepdims=True)
        p = jnp.exp(s - m)
        m_sc[j] = jnp.broadcast_to(m, (rows, LANES))
        l_sc[j] = jnp.broadcast_to(jnp.sum(p, axis=-1, keepdims=True), (rows, LANES))
        o_sc[j] = jnp.dot(p.astype(BF16), vb, preferred_element_type=F32)

    @pl.when(s_id == pl.num_programs(1) - 1)
    def _finish():
        gate = jnp.concatenate([lax.dot_general(head_rows(qf, h), km_sc[h], _NT, precision=lax.Precision.HIGHEST,
                                                preferred_element_type=F32) for h in range(B_KV_HEADS)], axis=0)
        col = lax.broadcasted_iota(I32, (rows, LANES), 1)
        gate = jnp.where(col < n_past_blocks, gate, NEG_INF)
        sel = _select_topk_lanes(gate, MOBA_TOPK)
        t_of_row = lax.broadcasted_iota(I32, (rows, 8), 0) % n_new
        new_of_head = lambda ref, h: ref[:, h * LANES:(h + 1) * LANES].astype(BF16)
        s_new = jnp.concatenate([lax.dot_general(head_rows(qb, h), new_of_head(kn_ref, h), _NT, preferred_element_type=F32)
                                 for h in range(B_KV_HEADS)], axis=0) * scale
        s_new = jnp.where(_new_key_mask(rows, n_new, t_of_row), s_new, NEG_INF)
        pv_new = lambda p: jnp.concatenate([jnp.dot(head_rows(p, h), new_of_head(vn_ref, h), preferred_element_type=F32)
                                            for h in range(B_KV_HEADS)], axis=0)
        m, l, acc = _softmax_step(_softmax_init(rows, LANES), s_new, pv_new)
        for j in range(n_past_blocks):
            sel_j = sel[:, j:j + 1]
            mj = jnp.where(sel_j, m_sc[j][:, 0:1], NEG_INF)
            m_new = jnp.maximum(m, mj)
            alpha = jnp.exp(m - m_new)
            beta = jnp.exp(mj - m_new)
            l = alpha * l + beta * l_sc[j][:, 0:1]
            acc = alpha * acc + beta * o_sc[j]
            m = m_new
        o = acc / l
        n_grp = B_HEADS // B_KV_HEADS
        for kvh in range(B_KV_HEADS):
            for g in range(n_grp):
                h = kvh * n_grp + g
                r0 = (kvh * n_grp + g) * n_new
                o_ref[:, h * LANES:(h + 1) * LANES] = o[r0:r0 + n_new]


def _decode_call(kernel, pt_flat, pre_inputs, pre_specs, cache_k, cache_v, scratch, *, bsz, n_new, n_steps, n_pages, name):
    page_specs = _paged_specs(n_pages, n_steps)
    return pl.pallas_call(
        kernel,
        grid_spec=pltpu.PrefetchScalarGridSpec(
            num_scalar_prefetch=1,
            grid=(bsz, n_steps),
            in_specs=pre_specs + page_specs + page_specs,
            out_specs=pl.BlockSpec((None, n_new, Q_W), lambda b, s, pt: (b, 0, 0)),
            scratch_shapes=scratch),
        out_shape=jax.ShapeDtypeStruct((bsz, n_new, Q_W), F32),
        compiler_params=_params(("arbitrary", "arbitrary")),
        name=name,
    )(pt_flat, *pre_inputs, *([cache_k] * n_pages), *([cache_v] * n_pages))


def _pad_new(x3):
    return jnp.pad(x3, ((0, 0), (0, 8 - x3.shape[1]), (0, 0)))


def _diff_attn_decode(qa, ka_new, va_new, cache_k, cache_v, page_table, prm, *, n_pages_per_step):
    bsz, n_new = qa.shape[:2]
    cache_kt = jnp.transpose(cache_k[0], (0, 2, 3, 4, 1)).reshape(cache_k.shape[1], PAGE_ROWS, LANES)
    cache_v = _token_major_page(cache_v)
    n_grp = A_HEADS // A_KV_HEADS
    q6 = qa.reshape(bsz, n_new, A_KV_HEADS, n_grp, 2, A_DH).transpose(0, 2, 4, 3, 1, 5)
    eye = jnp.eye(A_KV_HEADS * 2, dtype=qa.dtype).reshape(A_KV_HEADS, 2, 1, 1, A_KV_HEADS * 2, 1)
    qbd = (q6[:, :, :, :, :, None, :] * eye[None]).reshape(bsz, A_KV_HEADS * 2 * n_grp * n_new, KV_W).astype(BF16)
    rows = qbd.shape[1]
    n_steps = page_table.shape[1] // n_pages_per_step
    lam_spec = _const_spec((1, A_DH))
    per_b = lambda r, w: pl.BlockSpec((None, r, w), lambda b, s, pt: (b, 0, 0))
    return _decode_call(
        functools.partial(_diff_decode_kernel, n_pages=n_pages_per_step, n_new=n_new),
        page_table.reshape(-1),
        [prm["a_lq1"], prm["a_lk1"], prm["a_lq2"], prm["a_lk2"], qbd, _pad_new(ka_new), _pad_new(va_new), prm["a_subln"]],
        [lam_spec, lam_spec, lam_spec, lam_spec, per_b(rows, KV_W), per_b(8, KV_W), per_b(8, KV_W), _const_spec((1, LANES))],
        cache_kt, cache_v,
        [pltpu.VMEM((rows, 1), F32), pltpu.VMEM((rows, 1), F32), pltpu.VMEM((rows, LANES), F32)],
        bsz=bsz, n_new=n_new, n_steps=n_steps, n_pages=n_pages_per_step, name="diff_attn_decode")


def _moba_decode(qb, kb_new, vb_new, cache_k, cache_v, page_table, *, n_pages_per_step):
    bsz, n_new = qb.shape[:2]
    n_grp = B_HEADS // B_KV_HEADS
    rows = B_KV_HEADS * n_grp * n_new
    qrows = qb.reshape(bsz, n_new, B_KV_HEADS, n_grp, B_DH).transpose(0, 2, 3, 1, 4).reshape(bsz, rows, B_DH)
    n_steps = page_table.shape[1] // n_pages_per_step
    n_past_blocks = page_table.shape[1] * PAGE_SIZE // MOBA_BLOCK
    assert (page_table.shape[1] * PAGE_SIZE) % MOBA_BLOCK == 0 and n_past_blocks <= LANES
    per_b = lambda r, w: pl.BlockSpec((None, r, w), lambda b, s, pt: (b, 0, 0))
    return _decode_call(
        functools.partial(_moba_decode_kernel, n_pages=n_pages_per_step, n_new=n_new, n_past_blocks=n_past_blocks),
        page_table.reshape(-1),
        [qrows, _pad_new(kb_new), _pad_new(vb_new)],
        [per_b(rows, B_DH), per_b(8, KV_W), per_b(8, KV_W)],
        _token_major_page(cache_k), _token_major_page(cache_v),
        [pltpu.VMEM((n_past_blocks, rows, LANES), F32), pltpu.VMEM((n_past_blocks, rows, LANES), F32),
         pltpu.VMEM((n_past_blocks, rows, LANES), F32), pltpu.VMEM((B_KV_HEADS, LANES, B_DH), F32)],
        bsz=bsz, n_new=n_new, n_steps=n_steps, n_pages=n_pages_per_step, name="moba_decode")


def _topk_rows(s, k, payload=None):
    n_rows = s.shape[0]
    row = lax.broadcasted_iota(I32, s.shape, 0).astype(F32)
    vals, picks = [], []
    for _ in range(k):
        mx = jnp.max(s, axis=0, keepdims=True)
        ix = jnp.min(jnp.where(s == mx, row, float(n_rows)), axis=0, keepdims=True)
        hit = row == ix
        vals.append(mx)
        picks.append(ix if payload is None else jnp.sum(jnp.where(hit, payload, 0.0), axis=0, keepdims=True))
        s = jnp.where(hit, NEG_INF, s)
    return jnp.concatenate(vals, axis=0), jnp.concatenate(picks, axis=0)


def _merge_route_kernel(x_ref, oa_ref, ob_ref, om_ref, gt_ref, wa_ref, wb_ref, wm_ref, wo_ref, fg_ref, wq_ref, keys_ref,
                        x1_ref, h2_ref, eid_ref, gw_ref):
    ya = jnp.dot(oa_ref[...].astype(BF16), wa_ref[...], preferred_element_type=F32)
    yb = jnp.dot(ob_ref[...].astype(BF16), wb_ref[...], preferred_element_type=F32)
    ym = jnp.dot(om_ref[...].astype(BF16), wm_ref[...], preferred_element_type=F32)
    merged = (gt_ref[:, 0:D_MODEL] * ya + gt_ref[:, D_MODEL:2 * D_MODEL] * yb + gt_ref[:, 2 * D_MODEL:3 * D_MODEL] * ym)
    x1 = x_ref[...] + jnp.dot(merged.astype(BF16), wo_ref[...], preferred_element_type=F32)
    x1_ref[...] = x1
    h2 = _rms(x1, fg_ref[...], D_MODEL)
    h2_ref[...] = h2
    qp = jnp.dot(h2.astype(BF16), wq_ref[...], preferred_element_type=F32)
    half = PEER_DQ // 2
    for h in range(PEER_HEADS):
        sub = []
        for p in range(2):
            qhp = qp[:, (h * 2 + p) * half:(h * 2 + p + 1) * half].astype(BF16)
            st = lax.dot_general(keys_ref[h * 2 + p], qhp, _NT, preferred_element_type=F32)
            sub.append(_topk_rows(st, PEER_TOPK))
        (s1, i1), (s2, i2) = sub
        keep = [PEER_TOPK // (a + 1) for a in range(PEER_TOPK)]
        pad = -sum(keep) % 8
        cand_s = jnp.concatenate([s1[a:a + 1, :] + s2[0:keep[a], :] for a in range(PEER_TOPK)]
                                 + [jnp.full((pad, s1.shape[1]), NEG_INF, F32)], axis=0)
        cand_i = jnp.concatenate([i1[a:a + 1, :] * PEER_NKEYS + i2[0:keep[a], :] for a in range(PEER_TOPK)]
                                 + [jnp.zeros((pad, s1.shape[1]), F32)], axis=0)
        top_s, expert = _topk_rows(cand_s, PEER_TOPK, payload=cand_i)
        e = jnp.exp(top_s - top_s[0:1, :])
        gw_ref[h * PEER_TOPK:(h + 1) * PEER_TOPK, :] = e / jnp.sum(e, axis=0, keepdims=True)
        eid_ref[h * PEER_TOPK:(h + 1) * PEER_TOPK, :] = expert.astype(I32) * PEER_SLAB


def _merge_route(x2d, oa, ob, om, gates, prm, *, tm):
    n = x2d.shape[0]
    row = lambda w: pl.BlockSpec((tm, w), lambda i: (i, 0))
    colblk = pl.BlockSpec((PEER_HEADS * PEER_TOPK, tm), lambda i: (0, i))
    sq = _const_spec((D_MODEL, D_MODEL))
    return pl.pallas_call(
        _merge_route_kernel,
        grid=(n // tm,),
        in_specs=[row(D_MODEL), row(Q_W), row(Q_W), row(Q_W), row(GATE_W), sq, sq, sq, sq, _const_spec((1, D_MODEL)),
                  _const_spec((D_MODEL, PEER_HEADS * PEER_DQ)), _const_spec((PEER_HEADS * 2, PEER_NKEYS, PEER_DQ // 2))],
        out_specs=[row(D_MODEL), row(D_MODEL), colblk, colblk],
        out_shape=[jax.ShapeDtypeStruct((n, D_MODEL), F32), jax.ShapeDtypeStruct((n, D_MODEL), F32),
                   jax.ShapeDtypeStruct((PEER_HEADS * PEER_TOPK, n), I32), jax.ShapeDtypeStruct((PEER_HEADS * PEER_TOPK, n), F32)],
        compiler_params=_params(("arbitrary",)),
        name="merge_route",
    )(x2d, oa, ob, om, gates, prm["w_br_a"], prm["w_br_b"], prm["w_br_m"], prm["w_out"], prm["ffn_norm_g"],
      prm["w_peer_q"], prm["peer_keys"])


def _pack_table(tab):
    bits = lax.bitcast_convert_type(tab.astype(jnp.bfloat16), jnp.uint16).astype(jnp.uint32)
    half = tab.shape[1] // 2
    words = bits[:, :half] | (bits[:, half:] << 16)
    return lax.bitcast_convert_type(words, I32).reshape(tab.shape[0] * PEER_SLAB, LANES)


def _unpack_row(tab_ref, row0):
    slab = tab_ref[pl.ds(pl.multiple_of(row0, PEER_SLAB), PEER_SLAB), :]
    lo = lax.bitcast_convert_type(slab << 16, F32)
    hi = lax.bitcast_convert_type(slab & jnp.int32(-65536), F32)
    return lo, hi


def _peer_act_kernel(eid_ref, x_ref, gw_ref, tab_ref, w_ref, s_sc, z_sc, *, tt):
    n_k = PEER_HEADS * PEER_TOPK

    def products(t, carry):
        x8 = x_ref[t]
        xlo, xhi = x8[0:PEER_SLAB], x8[PEER_SLAB:2 * PEER_SLAB]
        experts = eid_ref.at[t]
        slots = s_sc.at[t]
        for k in range(n_k):
            lo, hi = _unpack_row(tab_ref, experts[k])
            slots[k * PEER_SLAB:(k + 1) * PEER_SLAB, :] = lo * xlo + hi * xhi
        return carry

    def lane_sums(i, carry):
        for u in range(PEER_XPOSE_UNROLL):
            t = i * PEER_XPOSE_UNROLL + u
            slots = s_sc.at[t]
            v = slots[...].reshape(n_k * PEER_SLAB // 8, 8, LANES)
            r = v + pltpu.roll(v, 8 - 2, 1)
            r = r + pltpu.roll(r, 8 - 1, 1)
            slots[...] = r.reshape(n_k * PEER_SLAB, LANES)
            per_expert = slots[pl.ds(0, n_k, stride=PEER_SLAB), :]
            z_sc[pl.ds(t, 1), :] = jnp.sum(per_expert.T, axis=0, keepdims=True)
        return carry

    lax.fori_loop(0, tt, products, 0)
    lax.fori_loop(0, tt // PEER_XPOSE_UNROLL, lane_sums, 0)
    a = z_sc[...]
    gelu = 0.5 * a * (1.0 + lax.erf(a * (2.0 ** -0.5)))
    w_ref[...] = gw_ref[...] * gelu


def _peer_out_kernel(eid_ref, w_ref, x1_ref, tab_ref, y_ref, wb_sc, *, tt):
    n_k = PEER_HEADS * PEER_TOPK

    def spread(i, carry):
        for u in range(PEER_XPOSE_UNROLL):
            t = i * PEER_XPOSE_UNROLL + u
            wb_sc[t] = jnp.broadcast_to(w_ref[pl.ds(t, 1), :], (n_k, n_k)).T
        return carry

    lax.fori_loop(0, tt // PEER_XPOSE_UNROLL, spread, 0)

    def token(t, carry):
        experts = eid_ref.at[t]
        weights = wb_sc.at[t]
        acc = [[jnp.zeros((PEER_SLAB, LANES), F32), jnp.zeros((PEER_SLAB, LANES), F32)] for _ in range(2)]
        for k in range(n_k):
            lo, hi = _unpack_row(tab_ref, experts[k])
            w = weights[k:k + 1, :]
            acc[k % 2][0] = acc[k % 2][0] + w * lo
            acc[k % 2][1] = acc[k % 2][1] + w * hi
        y_ref[t] = x1_ref[t] + jnp.concatenate([acc[0][0] + acc[1][0], acc[0][1] + acc[1][1]], axis=0)
        return carry

    lax.fori_loop(0, tt, token, 0)


def _peer(h2, x1, eid, gw, u_pack, v_pack, *, tt):
    n = h2.shape[0]
    n_k = PEER_HEADS * PEER_TOPK
    rows8 = D_MODEL // LANES
    smem = lambda: pl.BlockSpec((tt, n_k), lambda i: (i, 0), memory_space=pltpu.SMEM)
    tok3 = pl.BlockSpec((tt, rows8, LANES), lambda i: (i, 0, 0))
    tab_spec = _const_spec((PEER_N * PEER_SLAB, LANES), True)
    w = pl.pallas_call(
        functools.partial(_peer_act_kernel, tt=tt),
        grid=(n // tt,),
        in_specs=[smem(), tok3, pl.BlockSpec((tt, n_k), lambda i: (i, 0)), tab_spec],
        out_specs=pl.BlockSpec((tt, n_k), lambda i: (i, 0)),
        out_shape=jax.ShapeDtypeStruct((n, n_k), F32),
        scratch_shapes=[pltpu.VMEM((tt, n_k * PEER_SLAB, LANES), F32), pltpu.VMEM((tt, n_k), F32)],
        compiler_params=_params(("arbitrary",)),
        name="peer_act",
    )(eid, h2.reshape(n, rows8, LANES), gw, u_pack)
    y = pl.pallas_call(
        functools.partial(_peer_out_kernel, tt=tt),
        grid=(n // tt,),
        in_specs=[smem(), pl.BlockSpec((tt, n_k), lambda i: (i, 0)), tok3, tab_spec],
        out_specs=tok3,
        out_shape=jax.ShapeDtypeStruct((n, rows8, LANES), F32),
        scratch_shapes=[pltpu.VMEM((tt, n_k, n_k), F32)],
        compiler_params=_params(("arbitrary",)),
        name="peer_out",
    )(eid, w, x1.reshape(n, rows8, LANES), v_pack)
    return y.reshape(n, D_MODEL)


def _merge_ffn(x2d, oa, ob, om, gates, prm, u_pack, v_pack, *, tm, tt):
    x1, h2, eid_t, gw_t = _merge_route(x2d, oa, ob, om, gates, prm, tm=tm)
    return _peer(h2, x1, eid_t.T, gw_t.T, u_pack, v_pack, tt=tt)


def kernel(x_prompt, x_sample, mem_prompt, cache_a_k, cache_a_v, cache_b_k, cache_b_v, cache_mem_k, cache_mem_v, page_table, attn_norm_g, w_in, a_q_norm, a_k_norm, a_lq1, a_lk1, a_lq2, a_lk2, a_subln, b_q_norm, b_k_norm, mem_norm_g, w_mem_kv, m_q_norm, m_k_norm, w_br_a, w_br_b, w_br_m, w_out, ffn_norm_g, w_peer_q, peer_keys, peer_u, peer_v):
    assert attn_norm_g.shape[0] == 1, "single trunk layer"
    bsz, seq, _ = x_prompt.shape
    dbsz, n_new, _ = x_sample.shape
    n_mem = mem_prompt.shape[1]
    n_pool = cache_a_k.shape[1]
    past = page_table.shape[1] * PAGE_SIZE

    prm = dict(
        attn_norm_g=attn_norm_g, w_in=w_in[0].astype(BF16),
        a_q_norm=jnp.tile(a_q_norm, (1, 2)), a_k_norm=jnp.tile(a_k_norm, (1, 2)),
        a_lq1=a_lq1, a_lk1=a_lk1, a_lq2=a_lq2, a_lk2=a_lk2, a_subln=a_subln,
        b_q_norm=b_q_norm, b_k_norm=b_k_norm, mem_norm_g=mem_norm_g, w_mem_kv=w_mem_kv[0].astype(BF16),
        m_q_norm=m_q_norm, m_k_norm=m_k_norm,
        w_br_a=w_br_a[0].astype(BF16), w_br_b=w_br_b[0].astype(BF16), w_br_m=w_br_m[0].astype(BF16),
        w_out=w_out[0].astype(BF16), ffn_norm_g=ffn_norm_g, w_peer_q=w_peer_q[0].astype(BF16),
        peer_keys=peer_keys[0].reshape(PEER_HEADS * 2, PEER_NKEYS, PEER_DQ // 2).astype(BF16))
    u_pack = _pack_table(peer_u[0])
    v_pack = _pack_table(peer_v[0])

    tm = 256
    pos_p = jnp.arange(seq, dtype=I32)
    xp2 = x_prompt.reshape(bsz * seq, D_MODEL)
    qa, ka, va, qb, kb, vb, qm, gates = _in_proj(xp2, _rope_table(pos_p, A_DH), _rope_table(pos_p, B_DH), prm, tm=tm, q_dtype=BF16)
    r3 = lambda a: a.reshape(bsz, seq, a.shape[-1])
    oa = _diff_attn_prompt(r3(qa), r3(ka), r3(va), prm, tq=256)
    ob = _moba_prompt(r3(qb), r3(kb), r3(vb))
    mk, mv = _mem_kv(mem_prompt.reshape(bsz * n_mem, D_MODEL), prm, tm=tm)
    om = _mem_attn(r3(qm), mk.reshape(bsz, n_mem, Q_W), mv.reshape(bsz, n_mem, Q_W), tq=256, out_dtype=BF16)
    y_prompt = _merge_ffn(xp2, oa.reshape(-1, Q_W), ob.reshape(-1, Q_W), om.reshape(-1, Q_W), gates, prm, u_pack, v_pack, tm=tm, tt=32)

    pos_s = jnp.tile(past + jnp.arange(n_new, dtype=I32), tm // n_new)
    xs2 = x_sample.reshape(dbsz * n_new, D_MODEL)
    sqa, ska, sva, sqb, skb, svb, sqm, sgates = _in_proj(xs2, _rope_table(pos_s, A_DH), _rope_table(pos_s, B_DH), prm, tm=tm, q_dtype=F32)
    s3 = lambda a: a.reshape(dbsz, n_new, a.shape[-1])
    soa = _diff_attn_decode(s3(sqa), s3(ska), s3(sva), cache_a_k, cache_a_v, page_table, prm, n_pages_per_step=8)
    sob = _moba_decode(s3(sqb), s3(skb), s3(svb), cache_b_k, cache_b_v, page_table, n_pages_per_step=8)
    mem_rows = lambda c: (c[0].reshape(dbsz, n_mem, M_HEADS, M_DH // LANES, LANES).transpose(0, 1, 3, 2, 4)
                          .reshape(dbsz, n_mem * M_HEADS * (M_DH // LANES), LANES))
    som = _mem_attn(s3(sqm), mem_rows(cache_mem_k), mem_rows(cache_mem_v), tq=n_new, out_dtype=F32)
    y_sample = _merge_ffn(xs2, soa.reshape(-1, Q_W), sob.reshape(-1, Q_W), som.reshape(-1, Q_W), sgates, prm, u_pack, v_pack, tm=tm, tt=32)

    return (y_prompt.reshape(bsz, seq, D_MODEL), y_sample.reshape(dbsz, n_new, D_MODEL),
            ka.reshape(1, bsz, seq, A_KV_HEADS, 2, A_DH), va.reshape(1, bsz, seq, A_KV_HEADS, 2 * A_DH),
            kb.reshape(1, bsz, seq, B_KV_HEADS, B_DH), vb.reshape(1, bsz, seq, B_KV_HEADS, B_DH),
            mk.reshape(1, bsz, n_mem, M_HEADS, M_DH), mv.reshape(1, bsz, n_mem, M_HEADS, M_DH),
            ska.reshape(1, dbsz, n_new, A_KV_HEADS, 2, A_DH), sva.reshape(1, dbsz, n_new, A_KV_HEADS, 2 * A_DH),
            skb.reshape(1, dbsz, n_new, B_KV_HEADS, B_DH), svb.reshape(1, dbsz, n_new, B_KV_HEADS, B_DH))
```
